```python
import math
import jax
import jax.numpy as jnp
from jax import lax
import numpy as np

D_MODEL = 4096
BATCH = 1
SEQ = 16384
DEPTH = 4

MIX_WIDTH = D_MODEL
HEAD_DIM = 128
CONV_WIDTH = MIX_WIDTH // 2
CONV_K = 31
DN_WIDTH = MIX_WIDTH // 2
DN_HEAD_DIM = HEAD_DIM
DN_HEADS = DN_WIDTH // DN_HEAD_DIM
DN_CONV_K = 4
DN_CHUNK = 64
EVEN_SPLITS = (CONV_WIDTH, CONV_WIDTH, 3 * DN_WIDTH, DN_WIDTH, DN_HEADS, DN_HEADS)
EVEN_IN = sum(EVEN_SPLITS)
SC_WIDTH = MIX_WIDTH // 2
SC_K = 3
POOL_WIDTH = MIX_WIDTH // 2
POOL_WINDOWS = (2, 4, 8, 16)
POOL_GROUPS = len(POOL_WINDOWS)
POOL_GROUP_DIM = POOL_WIDTH // POOL_GROUPS
ODD_SPLITS = (SC_WIDTH, SC_WIDTH, SC_WIDTH, POOL_WIDTH)
ODD_IN = sum(ODD_SPLITS)
D_FF = 256 * ((8 * D_MODEL // 3 + 255) // 256)
FFN_CONV_K = 3
N_EVEN = (DEPTH + 1) // 2
N_ODD = DEPTH // 2
NORM_EPS = 1e-6

kernel_name = 'hybrid_conformer_deltanet_shortconv_pool_trunk'


def rms_norm(x, g):
    xf = x.astype(jnp.float32)
    y = xf * lax.rsqrt(jnp.mean(xf * xf, axis=-1, keepdims=True) + NORM_EPS)
    return (y * g.astype(jnp.float32)).astype(x.dtype)


def layer_norm(x, g, b):
    xf = x.astype(jnp.float32)
    xc = xf - jnp.mean(xf, axis=-1, keepdims=True)
    y = xc * lax.rsqrt(jnp.mean(xc * xc, axis=-1, keepdims=True) + NORM_EPS)
    return (y * g.astype(jnp.float32) + b.astype(jnp.float32)).astype(x.dtype)


def l2_normalize(x):
    return x * lax.rsqrt(jnp.sum(x * x, axis=-1, keepdims=True) + NORM_EPS)


def causal_dwconv(x, w):
    k, c = w.shape
    return lax.conv_general_dilated(
        x, w[:, None, :].astype(x.dtype), window_strides=(1,), padding=[(k - 1, 0)],
        dimension_numbers=('NWC', 'WIO', 'NWC'), feature_group_count=c)


def split_last(t, sizes):
    return jnp.split(t, [int(s) for s in np.cumsum(sizes)[:-1]], axis=-1)


def chunk_gated_delta_rule(q, k, v, beta, g):
    bsz, seq, heads, dk = q.shape
    dv = v.shape[-1]
    n_chunks = seq // DN_CHUNK

    def to_chunks(t):
        t = jnp.moveaxis(t, 2, 1)
        return t.reshape(t.shape[:2] + (n_chunks, DN_CHUNK) + t.shape[3:])

    q, k, v, beta, g = (to_chunks(t) for t in (q, k, v, beta, g))
    gc = jnp.cumsum(g, axis=-1)
    lower = jnp.tril(jnp.ones((DN_CHUNK, DN_CHUNK), dtype=bool))
    strict = jnp.tril(jnp.ones((DN_CHUNK, DN_CHUNK), dtype=bool), -1)
    diff = gc[..., :, None] - gc[..., None, :]
    decay_mat = jnp.where(lower, jnp.exp(jnp.where(lower, diff, 0.0)), 0.0)
    k_beta = k * beta[..., None]
    m = jnp.where(strict, jnp.einsum('bhncd,bhnsd->bhncs', k_beta, k) * decay_mat, 0.0)
    eye = jnp.eye(DN_CHUNK, dtype=jnp.float32)
    t_inv = lax.linalg.triangular_solve(eye + m, jnp.broadcast_to(eye, m.shape),
                                        left_side=True, lower=True, unit_diagonal=True)
    u = t_inv @ (v * beta[..., None])
    w = t_inv @ (k_beta * jnp.exp(gc)[..., None])
    a_qk = jnp.einsum('bhncd,bhnsd->bhncs', q, k) * decay_mat
    g_last = gc[..., -1]
    q_dec = q * jnp.exp(gc)[..., None]
    k_dec = k * jnp.exp(g_last[..., None] - gc)[..., None]

    def step(state, inp):
        q_i, k_i, u_i, w_i, a_i, gl_i = inp
        v_new = u_i - jnp.einsum('bhcd,bhde->bhce', w_i, state)
        o_i = jnp.einsum('bhcd,bhde->bhce', q_i, state) + jnp.einsum('bhcs,bhse->bhce', a_i, v_new)
        state = state * jnp.exp(gl_i)[..., None, None] + jnp.einsum('bhcd,bhce->bhde', k_i, v_new)
        return state, o_i

    xs = tuple(jnp.moveaxis(t, 2, 0) for t in (q_dec, k_dec, u, w, a_qk, g_last))
    state0 = jnp.zeros((bsz, heads, dk, dv), jnp.float32)
    _, o = lax.scan(step, state0, xs)
    return jnp.transpose(o, (1, 0, 3, 2, 4)).reshape(bsz, seq, heads, dv)


def even_mixer(h, w_in, w_out, a_conv_w, a_conv_b, a_ln_g, a_ln_b,
               dn_conv_w, dn_a_log, dn_dt_bias, dn_norm_g):
    bsz, seq, _ = h.shape
    f32 = jnp.float32
    a_val, a_gate, qkv, z, b_logit, a_logit = split_last(h @ w_in, EVEN_SPLITS)
    u = a_val * jax.nn.sigmoid(a_gate)
    u = causal_dwconv(u, a_conv_w) + a_conv_b.astype(u.dtype)
    u = jax.nn.silu(layer_norm(u, a_ln_g, a_ln_b))
    qkv = jax.nn.silu(causal_dwconv(qkv, dn_conv_w)).astype(f32)
    q, k, v = [t.reshape(bsz, seq, DN_HEADS, DN_HEAD_DIM) for t in jnp.split(qkv, 3, axis=-1)]
    q = l2_normalize(q) * (DN_HEAD_DIM ** -0.5)
    k = l2_normalize(k)
    beta = jax.nn.sigmoid(b_logit.astype(f32))
    g = -jnp.exp(dn_a_log.astype(f32)) * jax.nn.softplus(a_logit.astype(f32) + dn_dt_bias.astype(f32))
    o = chunk_gated_delta_rule(q, k, v, beta, g)
    zf = z.astype(f32).reshape(bsz, seq, DN_HEADS, DN_HEAD_DIM)
    o = (rms_norm(o, dn_norm_g) * jax.nn.silu(zf)).reshape(bsz, seq, DN_WIDTH).astype(h.dtype)
    return jnp.concatenate([u, o], axis=-1) @ w_out


def multiscale_pool(p, pool_w, pool_scale):
    bsz, seq, _ = p.shape
    pf = p.astype(jnp.float32)
    cs = jnp.cumsum(pf, axis=1)
    t_count = jnp.arange(1, seq + 1, dtype=jnp.float32)[:, None]
    groups = []
    for gi, win in enumerate(POOL_WINDOWS):
        sl = slice(gi * POOL_GROUP_DIM, (gi + 1) * POOL_GROUP_DIM)
        cs_g = cs[..., sl]
        lagged = jnp.pad(cs_g[:, :seq - win], ((0, 0), (win, 0), (0, 0)))
        groups.append((cs_g - lagged) / jnp.minimum(t_count, float(win)) - pf[..., sl])
    mixed = jnp.stack(groups, axis=2)
    mixed = jnp.einsum('btgc,gcd->btgd', mixed, pool_w.astype(jnp.float32))
    mixed = mixed * pool_scale.astype(jnp.float32).reshape(POOL_GROUPS, POOL_GROUP_DIM)
    return mixed.reshape(bsz, seq, POOL_WIDTH).astype(p.dtype)


def odd_mixer(h, w_in, w_out, sc_conv_w, pool_w, pool_scale):
    gate_b, gate_c, val, p = split_last(h @ w_in, ODD_SPLITS)
    c_out = gate_b * causal_dwconv(gate_c * val, sc_conv_w)
    d_out = multiscale_pool(p, pool_w, pool_scale)
    return jnp.concatenate([c_out, d_out], axis=-1) @ w_out


def conv_ffn(h, w_gate, conv_w, w_up, w_down):
    gate = causal_dwconv(h @ w_gate, conv_w)
    return (jax.nn.silu(gate) * (h @ w_up)) @ w_down


def setup_inputs(seed: int = 0) -> dict:
    key = jax.random.key(seed)
    ks = jax.random.split(key, 24)
    f32 = jnp.float32

    def nrm(k, shape, scale):
        return jax.random.normal(k, shape, f32) * scale

    def gain(k, shape):
        return 1.0 + nrm(k, shape, 0.02)

    dt = jnp.exp(jax.random.uniform(ks[12], (N_EVEN, DN_HEADS), f32)
                 * (math.log(0.1) - math.log(0.001)) + math.log(0.001))
    return {
        'x': nrm(ks[0], (BATCH, SEQ, D_MODEL), 1.0),
        'mix_norm_g': gain(ks[1], (DEPTH, D_MODEL)),
        'ffn_norm_g': gain(ks[2], (DEPTH, D_MODEL)),
        'final_norm_g': gain(ks[3], (D_MODEL,)),
        'ev_w_in': nrm(ks[4], (N_EVEN, D_MODEL, EVEN_IN), D_MODEL ** -0.5),
        'ev_w_out': nrm(ks[5], (N_EVEN, MIX_WIDTH, D_MODEL), MIX_WIDTH ** -0.5),
        'a_conv_w': nrm(ks[6], (N_EVEN, CONV_K, CONV_WIDTH), CONV_K ** -0.5),
        'a_conv_b': nrm(ks[7], (N_EVEN, CONV_WIDTH), 0.02),
        'a_ln_g': gain(ks[8], (N_EVEN, CONV_WIDTH)),
        'a_ln_b': nrm(ks[9], (N_EVEN, CONV_WIDTH), 0.02),
        'dn_conv_w': nrm(ks[10], (N_EVEN, DN_CONV_K, 3 * DN_WIDTH), DN_CONV_K ** -0.5),
        'dn_a_log': jnp.log(jax.random.uniform(ks[11], (N_EVEN, DN_HEADS), f32, 1.0, 16.0)),
        'dn_dt_bias': dt + jnp.log(-jnp.expm1(-dt)),
        'dn_norm_g': gain(ks[13], (N_EVEN, DN_HEAD_DIM)),
        'od_w_in': nrm(ks[14], (N_ODD, D_MODEL, ODD_IN), D_MODEL ** -0.5),
        'od_w_out': nrm(ks[15], (N_ODD, MIX_WIDTH, D_MODEL), MIX_WIDTH ** -0.5),
        'sc_conv_w': nrm(ks[16], (N_ODD, SC_K, SC_WIDTH), SC_K ** -0.5),
        'pool_w': nrm(ks[17], (N_ODD, POOL_GROUPS, POOL_GROUP_DIM, POOL_GROUP_DIM), POOL_GROUP_DIM ** -0.5),
        'pool_scale': gain(ks[18], (N_ODD, POOL_WIDTH)),
        'ffn_w_gate': nrm(ks[19], (DEPTH, D_MODEL, D_FF), D_MODEL ** -0.5),
        'ffn_conv_w': nrm(ks[20], (DEPTH, FFN_CONV_K, D_FF), FFN_CONV_K ** -0.5),
        'ffn_w_up': nrm(ks[21], (DEPTH, D_MODEL, D_FF), D_MODEL ** -0.5),
        'ffn_w_down': nrm(ks[22], (DEPTH, D_FF, D_MODEL), D_FF ** -0.5),
    }


def reference(x, mix_norm_g, ffn_norm_g, final_norm_g,
              ev_w_in, ev_w_out, a_conv_w, a_conv_b, a_ln_g, a_ln_b,
              dn_conv_w, dn_a_log, dn_dt_bias, dn_norm_g,
              od_w_in, od_w_out, sc_conv_w, pool_w, pool_scale,
              ffn_w_gate, ffn_conv_w, ffn_w_up, ffn_w_down):
    for layer in range(DEPTH):
        j = layer // 2
        h = rms_norm(x, mix_norm_g[layer])
        if layer % 2 == 0:
            x = x + even_mixer(h, ev_w_in[j], ev_w_out[j], a_conv_w[j], a_conv_b[j],
                               a_ln_g[j], a_ln_b[j], dn_conv_w[j], dn_a_log[j],
                               dn_dt_bias[j], dn_norm_g[j])
        else:
            x = x + odd_mixer(h, od_w_in[j], od_w_out[j], sc_conv_w[j], pool_w[j], pool_scale[j])
        h = rms_norm(x, ffn_norm_g[layer])
        x = x + conv_ffn(h, ffn_w_gate[layer], ffn_conv_w[layer], ffn_w_up[layer], ffn_w_down[layer])
    return rms_norm(x, final_norm_g)
```

```python
import functools

import jax
import jax.numpy as jnp
from jax import lax
from jax.experimental import pallas as pl
from jax.experimental.pallas import tpu as pltpu

F32 = jnp.float32
BF16 = jnp.bfloat16
NORM_EPS = 1e-6
LANES = 128
SUBLANES = 8
HEAD_DIM = 128
DN_BLOCK = 128
POOL_WINDOWS = (2, 4, 8, 16)
VMEM_LIMIT_BYTES = 56 * 1024 * 1024
FFN_TILE = 512


def _params(n_axes):
    return pltpu.CompilerParams(dimension_semantics=("arbitrary",) * n_axes,
                                vmem_limit_bytes=VMEM_LIMIT_BYTES)


def _silu(x):
    return x * jax.nn.sigmoid(x)


def _softplus(x):
    return jnp.maximum(x, 0.0) + jnp.log1p(jnp.exp(-jnp.abs(x)))


def _dot(a, b):
    return jnp.dot(a, b, preferred_element_type=F32)


def _dot_nt(a, b):
    return lax.dot_general(a, b, (((1,), (1,)), ((), ())), preferred_element_type=F32)


def _rmsnorm_kernel(x_ref, g_ref, o_ref):
    x = x_ref[...]
    ms = jnp.mean(x * x, axis=-1, keepdims=True)
    o_ref[...] = (x * lax.rsqrt(ms + NORM_EPS) * g_ref[...]).astype(o_ref.dtype)


def rmsnorm(x, g, out_dtype, tm=256):
    m, d = x.shape
    return pl.pallas_call(
        _rmsnorm_kernel,
        grid=(m // tm,),
        in_specs=[pl.BlockSpec((tm, d), lambda i: (i, 0)),
                  pl.BlockSpec((1, d), lambda i: (0, 0))],
        out_specs=pl.BlockSpec((tm, d), lambda i: (i, 0)),
        out_shape=jax.ShapeDtypeStruct((m, d), out_dtype),
        compiler_params=_params(1),
        name="rmsnorm",
    )(x, g.reshape(1, d).astype(F32))


def _matmul_kernel(a_ref, w_ref, o_ref):
    o_ref[...] = _dot(a_ref[...], w_ref[...]).astype(o_ref.dtype)


def matmul(a, w, out_dtype=F32, tm=1024, tn=512):
    m, k = a.shape
    n = w.shape[1]
    tm, tn = min(tm, m), min(tn, n)
    return pl.pallas_call(
        _matmul_kernel,
        grid=(m // tm, n // tn),
        in_specs=[pl.BlockSpec((tm, k), lambda i, j: (i, 0)),
                  pl.BlockSpec((k, tn), lambda i, j: (0, j))],
        out_specs=pl.BlockSpec((tm, tn), lambda i, j: (i, j)),
        out_shape=jax.ShapeDtypeStruct((m, n), out_dtype),
        compiler_params=_params(2),
        name="matmul",
    )(a, w)


def _matmul_residual_kernel(a_ref, w_ref, x_ref, o_ref):
    @pl.when(pl.program_id(2) == 0)
    def _():
        o_ref[...] = x_ref[...]

    o_ref[...] += _dot(a_ref[...], w_ref[...])


def matmul_residual(a, w, x, tm=1024, tn=2048, tk=1024):
    m, k = a.shape
    n = w.shape[1]
    tm, tn, tk = min(tm, m), min(tn, n), min(tk, k)
    return pl.pallas_call(
        _matmul_residual_kernel,
        grid=(n // tn, m // tm, k // tk),
        in_specs=[pl.BlockSpec((tm, tk), lambda j, i, kk: (i, kk)),
                  pl.BlockSpec((tk, tn), lambda j, i, kk: (kk, j)),
                  pl.BlockSpec((tm, tn), lambda j, i, kk: (i, j))],
        out_specs=pl.BlockSpec((tm, tn), lambda j, i, kk: (i, j)),
        out_shape=jax.ShapeDtypeStruct((m, n), F32),
        compiler_params=_params(3),
        name="matmul_residual",
    )(a, w, x)


def _ffn_act_kernel(h_ref, wg_ref, wu_ref, cw_ref, o_ref, gbuf, carry):
    i, j = pl.program_id(0), pl.program_id(1)
    tm = h_ref.shape[0]
    h = h_ref[...]
    gate = _dot(h, wg_ref[...])
    up = _dot(h, wu_ref[...])

    @pl.when(i == 0)
    def _():
        carry[j] = jnp.zeros(carry.shape[1:], F32)

    gbuf[pl.ds(0, SUBLANES), :] = carry[j]
    gbuf[pl.ds(SUBLANES, tm), :] = gate
    carry[j] = gbuf[pl.ds(tm, SUBLANES), :]
    cw = cw_ref[...]
    conv = (cw[2:3] * gate + cw[1:2] * gbuf[pl.ds(SUBLANES - 1, tm), :]
            + cw[0:1] * gbuf[pl.ds(SUBLANES - 2, tm), :])
    o_ref[...] = (_silu(conv) * up).astype(o_ref.dtype)


def ffn_act(h, w_gate, w_up, conv_w, tm=1024, tf=FFN_TILE):
    m, d = h.shape
    f = w_gate.shape[1]
    tm = min(tm, m)
    return pl.pallas_call(
        _ffn_act_kernel,
        grid=(m // tm, f // tf),
        in_specs=[pl.BlockSpec((tm, d), lambda i, j: (i, 0)),
                  pl.BlockSpec((d, tf), lambda i, j: (0, j)),
                  pl.BlockSpec((d, tf), lambda i, j: (0, j)),
                  pl.BlockSpec((conv_w.shape[0], tf), lambda i, j: (0, j))],
        out_specs=pl.BlockSpec((tm, tf), lambda i, j: (i, j)),
        out_shape=jax.ShapeDtypeStruct((m, f), BF16),
        scratch_shapes=[pltpu.VMEM((tm + SUBLANES, tf), F32),
                        pltpu.VMEM((f // tf, SUBLANES, tf), F32)],
        compiler_params=_params(2),
        name="ffn_act",
    )(h, w_gate, w_up, conv_w)


CONV_HALO = 32


def _mixer_a_kernel(av_ref, ag_ref, cw_ref, cb_ref, lg_ref, lb_ref, o_ref, ubuf, cbuf, *, taps, row_chunk):
    i = pl.program_id(0)
    tt, c = av_ref.shape

    @pl.when(i == 0)
    def _():
        ubuf[pl.ds(0, CONV_HALO), :] = jnp.zeros((CONV_HALO, c), F32)

    @pl.when(i > 0)
    def _():
        ubuf[pl.ds(0, CONV_HALO), :] = ubuf[pl.ds(tt, CONV_HALO), :]

    ubuf[pl.ds(CONV_HALO, tt), :] = av_ref[...] * jax.nn.sigmoid(ag_ref[...])

    base = CONV_HALO - (taps - 1)

    def lane_chunk(l, _):
        l0 = pl.multiple_of(l * LANES, LANES)
        for r0 in range(0, tt, row_chunk):
            acc = jnp.broadcast_to(cb_ref[:, pl.ds(l0, LANES)], (row_chunk, LANES))
            for j in range(taps):
                acc = acc + cw_ref[j:j + 1, pl.ds(l0, LANES)] * ubuf[pl.ds(r0 + base + j, row_chunk), pl.ds(l0, LANES)]
            cbuf[pl.ds(r0, row_chunk), pl.ds(l0, LANES)] = acc
        return 0

    lax.fori_loop(0, c // LANES, lane_chunk, 0)

    def row_block(r, _):
        r0 = pl.multiple_of(r * row_chunk, row_chunk)
        y = cbuf[pl.ds(r0, row_chunk), :]
        yc = y - jnp.mean(y, axis=-1, keepdims=True)
        var = jnp.mean(yc * yc, axis=-1, keepdims=True)
        yn = yc * lax.rsqrt(var + NORM_EPS) * lg_ref[...] + lb_ref[...]
        o_ref[pl.ds(r0, row_chunk), :] = _silu(yn).astype(o_ref.dtype)
        return 0

    lax.fori_loop(0, tt // row_chunk, row_block, 0)


def mixer_a(proj, conv_w, conv_b, ln_g, ln_b, width, tt=256, row_chunk=64):
    m = proj.shape[0]
    taps = conv_w.shape[0]
    assert taps - 1 <= CONV_HALO
    tt = min(tt, m)
    row_chunk = min(row_chunk, tt)
    vec = lambda v: v.reshape(1, width).astype(F32)
    const = lambda i: (0, 0)
    return pl.pallas_call(
        functools.partial(_mixer_a_kernel, taps=taps, row_chunk=row_chunk),
        grid=(m // tt,),
        in_specs=[pl.BlockSpec((tt, width), lambda i: (i, 0)),
                  pl.BlockSpec((tt, width), lambda i: (i, 1)),
                  pl.BlockSpec((taps, width), const),
                  pl.BlockSpec((1, width), const),
                  pl.BlockSpec((1, width), const),
                  pl.BlockSpec((1, width), const)],
        out_specs=pl.BlockSpec((tt, width), lambda i: (i, 0)),
        out_shape=jax.ShapeDtypeStruct((m, width), BF16),
        scratch_shapes=[pltpu.VMEM((tt + CONV_HALO, width), F32),
                        pltpu.VMEM((tt, width), F32)],
        compiler_params=_params(1),
        name="mixer_a",
    )(proj, proj, conv_w.astype(F32), vec(conv_b), vec(ln_g), vec(ln_b))


def _deltanet_kernel(q_ref, k_ref, v_ref, z_ref, lg_ref, cwq_ref, cwk_ref, cwv_ref, alog_ref, dtb_ref, ng_ref,
                     o_ref, qbuf, kbuf, vbuf, state, *, taps, heads):
    i = pl.program_id(0)
    c = q_ref.shape[0]
    halo = SUBLANES
    bufs = ((q_ref, qbuf, cwq_ref), (k_ref, kbuf, cwk_ref), (v_ref, vbuf, cwv_ref))

    @pl.when(i == 0)
    def _():
        state[...] = jnp.zeros(state.shape, F32)
        for _, buf, _ in bufs:
            buf[pl.ds(0, halo), :] = jnp.zeros((halo, buf.shape[1]), F32)

    @pl.when(i > 0)
    def _():
        for _, buf, _ in bufs:
            buf[pl.ds(0, halo), :] = buf[pl.ds(c, halo), :]

    for ref, buf, _ in bufs:
        buf[pl.ds(halo, c), :] = ref[...]

    row = lax.broadcasted_iota(jnp.int32, (c, c), 0)
    col = lax.broadcasted_iota(jnp.int32, (c, c), 1)
    lower = row >= col
    strict = row > col
    eye = (row == col).astype(F32)

    beta_all = jax.nn.sigmoid(lg_ref[:, 0:LANES])
    g_all = -jnp.exp(alog_ref[...]) * _softplus(lg_ref[:, LANES:2 * LANES] + dtb_ref[...])
    tri = lower.astype(BF16)
    g_hi = g_all.astype(BF16)
    r1 = g_all - g_hi.astype(F32)
    g_mid = r1.astype(BF16)
    g_lo = (r1 - g_mid.astype(F32)).astype(BF16)
    gc_all = _dot(tri, g_hi) + _dot(tri, g_mid) + _dot(tri, g_lo)
    gc_t = gc_all.T

    def conv_silu(buf, cw_ref, hs):
        acc = cw_ref[0:1, hs] * buf[pl.ds(halo - (taps - 1), c), hs]
        for j in range(1, taps):
            acc = acc + cw_ref[j:j + 1, hs] * buf[pl.ds(halo - (taps - 1) + j, c), hs]
        return _silu(acc)

    for h in range(heads):
        hs = slice(h * HEAD_DIM, (h + 1) * HEAD_DIM)
        qh = conv_silu(qbuf, cwq_ref, hs)
        kh = conv_silu(kbuf, cwk_ref, hs)
        vh = conv_silu(vbuf, cwv_ref, hs)
        qn = qh * lax.rsqrt(jnp.sum(qh * qh, axis=-1, keepdims=True) + NORM_EPS) * (HEAD_DIM ** -0.5)
        kn = kh * lax.rsqrt(jnp.sum(kh * kh, axis=-1, keepdims=True) + NORM_EPS)
        beta = beta_all[:, h:h + 1]
        gcol = gc_all[:, h:h + 1]
        grow = gc_t[h:h + 1, :]
        glast = gc_all[c - 1:c, h:h + 1]
        decay = jnp.where(lower, jnp.exp(jnp.where(lower, gcol - grow, 0.0)), 0.0)
        kb = kn * beta
        vb = vh * beta
        eg = jnp.exp(gcol)
        a1 = _dot_nt(jnp.concatenate([kb, qn], axis=0).astype(BF16), kn.astype(BF16))
        neg_m = jnp.where(strict, -(a1[:c] * decay), 0.0)
        aqk = a1[c:] * decay
        p = eye + neg_m
        pw = _dot(neg_m.astype(BF16), neg_m.astype(BF16))
        span = 2
        while 2 * span < c:
            r = _dot(jnp.concatenate([p, pw], axis=0).astype(BF16), pw.astype(BF16))
            p = p + r[:c]
            pw = r[c:]
            span *= 2
        p = p + _dot(p.astype(BF16), pw.astype(BF16))
        uw = _dot(p.astype(BF16), jnp.concatenate([vb, kb * eg], axis=1).astype(BF16))
        u = uw[:, :HEAD_DIM]
        w = uw[:, HEAD_DIM:]
        s = state[h]
        b1 = _dot(jnp.concatenate([w, qn * eg], axis=0).astype(BF16), s.astype(BF16))
        v_new = u - b1[:c]
        k_dec = kn * jnp.exp(glast - gcol)
        b2 = _dot(jnp.concatenate([aqk, k_dec.T], axis=0).astype(BF16), v_new.astype(BF16))
        o = b1[c:] + b2[:c]
        state[h] = s * jnp.exp(glast) + b2[c:]
        on = o * lax.rsqrt(jnp.mean(o * o, axis=-1, keepdims=True) + NORM_EPS) * ng_ref[...]
        o_ref[:, hs] = (on * _silu(z_ref[:, hs])).astype(o_ref.dtype)


def deltanet(proj, logits, conv_w, a_log, dt_bias, norm_g, width, col0):
    m = proj.shape[0]
    heads = width // HEAD_DIM
    taps = conv_w.shape[0]
    assert DN_BLOCK == HEAD_DIM and heads <= LANES and taps - 1 <= SUBLANES and m % DN_BLOCK == 0
    c = DN_BLOCK
    pad = lambda v: jnp.pad(v.astype(F32), (0, LANES - heads)).reshape(1, LANES)
    const = lambda i: (0, 0)
    blk = lambda cb: pl.BlockSpec((c, width), lambda i, cb=cb: (i, cb))
    return pl.pallas_call(
        functools.partial(_deltanet_kernel, taps=taps, heads=heads),
        grid=(m // c,),
        in_specs=[blk(col0), blk(col0 + 1), blk(col0 + 2), blk(col0 + 3),
                  pl.BlockSpec((c, 2 * LANES), lambda i: (i, 0)),
                  pl.BlockSpec((taps, width), lambda i: (0, 0)),
                  pl.BlockSpec((taps, width), lambda i: (0, 1)),
                  pl.BlockSpec((taps, width), lambda i: (0, 2)),
                  pl.BlockSpec((1, LANES), const),
                  pl.BlockSpec((1, LANES), const),
                  pl.BlockSpec((1, HEAD_DIM), const)],
        out_specs=pl.BlockSpec((c, width), lambda i: (i, 0)),
        out_shape=jax.ShapeDtypeStruct((m, width), BF16),
        scratch_shapes=[pltpu.VMEM((c + SUBLANES, width), F32)] * 3
                       + [pltpu.VMEM((heads, HEAD_DIM, HEAD_DIM), F32)],
        compiler_params=_params(1),
        name="deltanet",
    )(proj, proj, proj, proj, logits, conv_w.astype(F32), conv_w.astype(F32), conv_w.astype(F32),
      pad(a_log), pad(dt_bias), norm_g.reshape(1, HEAD_DIM).astype(F32))


POOL_HALO = 16


def _odd_mix_kernel(gb_ref, gc_ref, val_ref, p_ref, cw_ref, pw_ref, ps_ref, o_ref, cvbuf, pbuf, *, taps):
    i = pl.program_id(0)
    tt, cw_width = gb_ref.shape
    pool_width = p_ref.shape[1]
    halo = SUBLANES

    @pl.when(i == 0)
    def _():
        cvbuf[pl.ds(0, halo), :] = jnp.zeros((halo, cw_width), F32)
        pbuf[pl.ds(0, POOL_HALO), :] = jnp.zeros((POOL_HALO, pool_width), F32)

    @pl.when(i > 0)
    def _():
        cvbuf[pl.ds(0, halo), :] = cvbuf[pl.ds(tt, halo), :]
        pbuf[pl.ds(0, POOL_HALO), :] = pbuf[pl.ds(tt, POOL_HALO), :]

    cvbuf[pl.ds(halo, tt), :] = gc_ref[...] * val_ref[...]
    pbuf[pl.ds(POOL_HALO, tt), :] = p_ref[...]

    conv = cw_ref[0:1, :] * cvbuf[pl.ds(halo - (taps - 1), tt), :]
    for j in range(1, taps):
        conv = conv + cw_ref[j:j + 1, :] * cvbuf[pl.ds(halo - (taps - 1) + j, tt), :]
    o_ref[:, 0:cw_width] = (gb_ref[...] * conv).astype(o_ref.dtype)

    groups = len(POOL_WINDOWS)
    gd = pool_width // groups
    t_count = (i * tt + 1 + lax.broadcasted_iota(jnp.int32, (tt, 1), 0)).astype(F32)
    for g, win in enumerate(POOL_WINDOWS):
        gs = slice(g * gd, (g + 1) * gd)
        self_rows = pbuf[pl.ds(POOL_HALO, tt), gs]
        acc = self_rows
        for back in range(1, win):
            acc = acc + pbuf[pl.ds(POOL_HALO - back, tt), gs]
        mixed = acc / jnp.minimum(t_count, float(win)) - self_rows
        d = _dot(mixed.astype(BF16), pw_ref[g]) * ps_ref[:, gs]
        o_ref[:, cw_width + g * gd:cw_width + (g + 1) * gd] = d.astype(o_ref.dtype)


def odd_mix(proj, conv_w, pool_w, pool_scale, width, tt=256):
    m = proj.shape[0]
    taps = conv_w.shape[0]
    groups, gd, _ = pool_w.shape
    assert groups == len(POOL_WINDOWS) and max(POOL_WINDOWS) - 1 <= POOL_HALO and taps - 1 <= SUBLANES
    tt = min(tt, m)
    const2 = lambda i: (0, 0)
    return pl.pallas_call(
        functools.partial(_odd_mix_kernel, taps=taps),
        grid=(m // tt,),
        in_specs=[pl.BlockSpec((tt, width), lambda i: (i, 0)),
                  pl.BlockSpec((tt, width), lambda i: (i, 1)),
                  pl.BlockSpec((tt, width), lambda i: (i, 2)),
                  pl.BlockSpec((tt, width), lambda i: (i, 3)),
                  pl.BlockSpec((taps, width), const2),
                  pl.BlockSpec((groups, gd, gd), lambda i: (0, 0, 0)),
                  pl.BlockSpec((1, width), const2)],
        out_specs=pl.BlockSpec((tt, 2 * width), lambda i: (i, 0)),
        out_shape=jax.ShapeDtypeStruct((m, 2 * width), BF16),
        scratch_shapes=[pltpu.VMEM((tt + SUBLANES, width), F32),
                        pltpu.VMEM((tt + POOL_HALO, width), F32)],
        compiler_params=_params(1),
        name="odd_mix",
    )(proj, proj, proj, proj, conv_w.astype(F32), pool_w.astype(BF16), pool_scale.reshape(1, width).astype(F32))


def kernel(x, mix_norm_g, ffn_norm_g, final_norm_g, ev_w_in, ev_w_out, a_conv_w, a_conv_b, a_ln_g, a_ln_b,
           dn_conv_w, dn_a_log, dn_dt_bias, dn_norm_g, od_w_in, od_w_out, sc_conv_w, pool_w, pool_scale,
           ffn_w_gate, ffn_conv_w, ffn_w_up, ffn_w_down):
    bsz, seq, d_model = x.shape
    assert bsz == 1, "sequence mixers carry state along the row axis; one sequence per call"
    depth = mix_norm_g.shape[0]
    half = d_model // 2
    heads = dn_a_log.shape[1]
    d_ff = ffn_w_gate.shape[2]
    ff_pad = (-d_ff) % FFN_TILE
    xs = x.reshape(seq, d_model).astype(F32)

    for layer in range(depth):
        j = layer // 2
        h = rmsnorm(xs, mix_norm_g[layer], BF16)
        if layer % 2 == 0:
            w_in = ev_w_in[j].astype(BF16)
            proj = matmul(h, w_in[:, :6 * half])
            w_log = w_in[:, 6 * half:]
            w_log = jnp.concatenate([jnp.pad(w_log[:, :heads], ((0, 0), (0, LANES - heads))),
                                     jnp.pad(w_log[:, heads:], ((0, 0), (0, LANES - heads)))], axis=1)
            logits = matmul(h, w_log)
            ua = mixer_a(proj, a_conv_w[j], a_conv_b[j], a_ln_g[j], a_ln_b[j], half)
            ob = deltanet(proj, logits, dn_conv_w[j], dn_a_log[j], dn_dt_bias[j], dn_norm_g[j], half, col0=2)
            mixed = jnp.concatenate([ua, ob], axis=1)
            xs = matmul_residual(mixed, ev_w_out[j].astype(BF16), xs)
        else:
            proj = matmul(h, od_w_in[j].astype(BF16))
            mixed = odd_mix(proj, sc_conv_w[j], pool_w[j], pool_scale[j], half)
            xs = matmul_residual(mixed, od_w_out[j].astype(BF16), xs)
        h = rmsnorm(xs, ffn_norm_g[layer], BF16)
        w_gate = jnp.pad(ffn_w_gate[layer].astype(BF16), ((0, 0), (0, ff_pad)))
        w_up = jnp.pad(ffn_w_up[layer].astype(BF16), ((0, 0), (0, ff_pad)))
        w_down = jnp.pad(ffn_w_down[layer].astype(BF16), ((0, ff_pad), (0, 0)))
        conv_w = jnp.pad(ffn_conv_w[layer].astype(F32), ((0, 0), (0, ff_pad)))
        act = ffn_act(h, w_gate, w_up, conv_w)
        xs = matmul_residual(act, w_down, xs)
    return rmsnorm(xs, final_norm_g, x.dtype).reshape(bsz, seq, d_model)
```

```python
import functools

import jax
import jax.numpy as jnp
from jax import lax
from jax.experimental import pallas as pl
from jax.experimental.pallas import tpu as pltpu

F32 = jnp.float32
BF16 = jnp.bfloat16
NORM_EPS = 1e-6
LANES = 128
SUBLANES = 8
MXU_WIDTH = 256
HEAD_DIM = 128
DN_BLOCK = 128
POOL_WINDOWS = (2, 4, 8, 16)
VMEM_LIMIT_BYTES = 60000 * 1024
FFN_TILE = 512
ROW_TILE = 1024


def _params(n_axes):
    return pltpu.CompilerParams(dimension_semantics=("arbitrary",) * n_axes,
                                vmem_limit_bytes=VMEM_LIMIT_BYTES)


def _silu(x):
    return x * jax.nn.sigmoid(x)


def _softplus(x):
    return jnp.maximum(x, 0.0) + jnp.log1p(jnp.exp(-jnp.abs(x)))


def _dot(a, b):
    return jnp.dot(a, b, preferred_element_type=F32)


def _dot_nt(a, b):
    return lax.dot_general(a, b, (((1,), (1,)), ((), ())), preferred_element_type=F32)


def _lane_chunks(v):
    return [v[:, c0:c0 + LANES] for c0 in range(0, v.shape[1], LANES)]


def _scale_rows(v, r):
    return jnp.concatenate([ch * r for ch in _lane_chunks(v)], axis=1)


def _row_scale(ssq_ref, d_model):
    tot = jnp.sum(ssq_ref[...], axis=0)
    ms = jnp.sum(tot, axis=-1, keepdims=True) * (1.0 / d_model)
    return jnp.broadcast_to(lax.rsqrt(ms + NORM_EPS), tot.shape)


def _emit_norm_inputs(x_new, g_ref, xg_ref):
    xg_ref[...] = (x_new * g_ref[...]).astype(xg_ref.dtype)
    return functools.reduce(lambda a, b: a + b, _lane_chunks(x_new * x_new))


def _shift_rows(ext, shift, halo):
    return pltpu.roll(ext, shift, axis=0)[halo:]


def _rmsnorm_kernel(x_ref, g_ref, o_ref):
    x = x_ref[...]
    ms = jnp.mean(x * x, axis=-1, keepdims=True)
    o_ref[...] = (x * lax.rsqrt(ms + NORM_EPS) * g_ref[...]).astype(o_ref.dtype)


def rmsnorm(x, g, out_dtype, tm=256):
    m, d = x.shape
    return pl.pallas_call(
        _rmsnorm_kernel,
        grid=(m // tm,),
        in_specs=[pl.BlockSpec((tm, d), lambda i: (i, 0)),
                  pl.BlockSpec((1, d), lambda i: (0, 0))],
        out_specs=pl.BlockSpec((tm, d), lambda i: (i, 0)),
        out_shape=jax.ShapeDtypeStruct((m, d), out_dtype),
        compiler_params=_params(1),
        name="rmsnorm",
    )(x, g.reshape(1, d).astype(F32))


def _norm_inputs_kernel(x_ref, g_ref, xg_ref, ssq_ref):
    ssq_ref[...] = _emit_norm_inputs(x_ref[...], g_ref, xg_ref)


def norm_inputs(x, g, tm=256):
    m, d = x.shape
    return pl.pallas_call(
        _norm_inputs_kernel,
        grid=(m // tm,),
        in_specs=[pl.BlockSpec((tm, d), lambda i: (i, 0)),
                  pl.BlockSpec((1, d), lambda i: (0, 0))],
        out_specs=[pl.BlockSpec((tm, d), lambda i: (i, 0)),
                   pl.BlockSpec((None, tm, LANES), lambda i: (0, i, 0))],
        out_shape=[jax.ShapeDtypeStruct((m, d), BF16),
                   jax.ShapeDtypeStruct((1, m, LANES), F32)],
        compiler_params=_params(1),
        name="norm_inputs",
    )(x, g.reshape(1, d).astype(F32))


def _proj_kernel(xg_ref, ssq_ref, w_ref, o_ref, r_s):
    @pl.when(pl.program_id(1) == 0)
    def _():
        r_s[...] = _row_scale(ssq_ref, xg_ref.shape[1])

    o_ref[...] = _scale_rows(_dot(xg_ref[...], w_ref[...]), r_s[...]).astype(o_ref.dtype)


def proj_matmul(xg, ssq, w, layer, n_cols, tn, tm=ROW_TILE):
    m, k = xg.shape
    tm, tn = min(tm, m), min(tn, n_cols)
    parts = ssq.shape[0]
    return pl.pallas_call(
        _proj_kernel,
        grid=(m // tm, n_cols // tn),
        in_specs=[pl.BlockSpec((tm, k), lambda i, j: (i, 0)),
                  pl.BlockSpec((parts, tm, LANES), lambda i, j: (0, i, 0)),
                  pl.BlockSpec((None, k, tn), lambda i, j: (layer, 0, j))],
        out_specs=pl.BlockSpec((tm, tn), lambda i, j: (i, j)),
        out_shape=jax.ShapeDtypeStruct((m, n_cols), F32),
        scratch_shapes=[pltpu.VMEM((tm, LANES), F32)],
        compiler_params=_params(2),
        name="proj_matmul",
    )(xg, ssq, w)


def _out_proj_kernel(a_ref, w_ref, x_ref, g_ref, o_ref, xg_ref, ssq_ref):
    j = pl.program_id(1)
    x_new = x_ref[...] + _dot(a_ref[...], w_ref[...])
    o_ref[...] = x_new
    part = _emit_norm_inputs(x_new, g_ref, xg_ref)

    @pl.when(j == 0)
    def _():
        ssq_ref[...] = part

    @pl.when(j > 0)
    def _():
        ssq_ref[...] += part


def out_proj(a, w, layer, x, g_next, tm=ROW_TILE, tn=512):
    m, k = a.shape
    n = w.shape[2]
    tm, tn = min(tm, m), min(tn, n)
    return pl.pallas_call(
        _out_proj_kernel,
        grid=(m // tm, n // tn),
        in_specs=[pl.BlockSpec((tm, k), lambda i, j: (i, 0)),
                  pl.BlockSpec((None, k, tn), lambda i, j: (layer, 0, j)),
                  pl.BlockSpec((tm, tn), lambda i, j: (i, j)),
                  pl.BlockSpec((1, tn), lambda i, j: (0, j))],
        out_specs=[pl.BlockSpec((tm, tn), lambda i, j: (i, j)),
                   pl.BlockSpec((tm, tn), lambda i, j: (i, j)),
                   pl.BlockSpec((None, tm, LANES), lambda i, j: (0, i, 0))],
        out_shape=[jax.ShapeDtypeStruct((m, n), F32),
                   jax.ShapeDtypeStruct((m, n), BF16),
                   jax.ShapeDtypeStruct((1, m, LANES), F32)],
        compiler_params=_params(2),
        name="out_proj",
    )(a, w, x, g_next.reshape(1, n).astype(F32))


def _ffn_down_kernel(a_ref, w_ref, x_ref, g_ref, o_ref, *norm_refs):
    kk = pl.program_id(2)

    @pl.when(kk == 0)
    def _():
        o_ref[...] = x_ref[...]

    o_ref[...] += _dot(a_ref[...], w_ref[...])

    if norm_refs:
        xg_ref, ssq_ref = norm_refs

        @pl.when(kk == pl.num_programs(2) - 1)
        def _():
            ssq_ref[...] = _emit_norm_inputs(o_ref[...], g_ref, xg_ref)


def ffn_down(a, w, layer, x, g_next, tm=ROW_TILE, tn=2048, tk=1024):
    m, k = a.shape
    n = w.shape[2]
    tm, tn, tk = min(tm, m), min(tn, n), min(tk, k)
    emit = g_next is not None
    g = (g_next if emit else jnp.ones((n,), F32)).reshape(1, n).astype(F32)
    blk = pl.BlockSpec((tm, tn), lambda j, i, kk: (i, j))
    out_specs = [blk]
    out_shape = [jax.ShapeDtypeStruct((m, n), F32)]
    if emit:
        out_specs += [blk, pl.BlockSpec((None, tm, LANES), lambda j, i, kk: (j, i, 0))]
        out_shape += [jax.ShapeDtypeStruct((m, n), BF16), jax.ShapeDtypeStruct((n // tn, m, LANES), F32)]
    res = pl.pallas_call(
        _ffn_down_kernel,
        grid=(n // tn, m // tm, k // tk),
        in_specs=[pl.BlockSpec((tm, tk), lambda j, i, kk: (i, kk)),
                  pl.BlockSpec((None, tk, tn), lambda j, i, kk: (layer, kk, j)),
                  blk,
                  pl.BlockSpec((1, tn), lambda j, i, kk: (0, j))],
        out_specs=out_specs,
        out_shape=out_shape,
        compiler_params=_params(3),
        name="ffn_down",
    )(a, w, x, g)
    return res if emit else (res[0], None, None)


def _ffn_act_kernel(xg_ref, ssq_ref, wg_ref, wu_ref, cw_ref, o_ref, carry, r_s, *, sub):
    i, j = pl.program_id(0), pl.program_id(1)
    tm = xg_ref.shape[0]
    tf = o_ref.shape[1]
    taps = cw_ref.shape[0]

    @pl.when(j == 0)
    def _():
        r_s[...] = _row_scale(ssq_ref, xg_ref.shape[1])

    @pl.when(i == 0)
    def _():
        carry[j] = jnp.zeros(carry.shape[1:], F32)

    xg = xg_ref[...]
    r = r_s[...]
    for c0 in range(0, tf, sub):
        cs = slice(c0, c0 + sub)
        gate = _scale_rows(_dot(xg, wg_ref[:, cs]), r)
        up = _scale_rows(_dot(xg, wu_ref[:, cs]), r)
        ext = jnp.concatenate([carry[j, :, cs], gate], axis=0)
        carry[j, :, cs] = gate[tm - SUBLANES:, :]
        conv = cw_ref[taps - 1:taps, cs] * gate
        for s in range(1, taps):
            conv = conv + cw_ref[taps - 1 - s:taps - s, cs] * _shift_rows(ext, s, SUBLANES)
        o_ref[:, cs] = (_silu(conv) * up).astype(o_ref.dtype)


def ffn_act(xg, ssq, w_gate, w_up, conv_w, layer, tm=ROW_TILE, tf=FFN_TILE):
    m, d = xg.shape
    f = w_gate.shape[2]
    taps = conv_w.shape[1]
    tm = min(tm, m)
    parts = ssq.shape[0]
    assert taps - 1 <= SUBLANES
    return pl.pallas_call(
        functools.partial(_ffn_act_kernel, sub=MXU_WIDTH),
        grid=(m // tm, f // tf),
        in_specs=[pl.BlockSpec((tm, d), lambda i, j: (i, 0)),
                  pl.BlockSpec((parts, tm, LANES), lambda i, j: (0, i, 0)),
                  pl.BlockSpec((None, d, tf), lambda i, j: (layer, 0, j)),
                  pl.BlockSpec((None, d, tf), lambda i, j: (layer, 0, j)),
                  pl.BlockSpec((None, taps, tf), lambda i, j: (layer, 0, j))],
        out_specs=pl.BlockSpec((tm, tf), lambda i, j: (i, j)),
        out_shape=jax.ShapeDtypeStruct((m, f), BF16),
        scratch_shapes=[pltpu.VMEM((f // tf, SUBLANES, tf), F32),
                        pltpu.VMEM((tm, LANES), F32)],
        compiler_params=_params(2),
        name="ffn_act",
    )(xg, ssq, w_gate, w_up, conv_w)


CONV_HALO = 32


def _mixer_a_kernel(av_ref, ag_ref, cw_ref, cb_ref, lg_ref, lb_ref, o_ref, ubuf, cbuf, *, taps, row_chunk):
    i = pl.program_id(0)
    tt, c = av_ref.shape

    @pl.when(i == 0)
    def _():
        ubuf[pl.ds(0, CONV_HALO), :] = jnp.zeros((CONV_HALO, c), F32)

    @pl.when(i > 0)
    def _():
        ubuf[pl.ds(0, CONV_HALO), :] = ubuf[pl.ds(tt, CONV_HALO), :]

    ubuf[pl.ds(CONV_HALO, tt), :] = av_ref[...] * jax.nn.sigmoid(ag_ref[...])

    def lane_chunk(l, _):
        l0 = pl.multiple_of(l * LANES, LANES)
        w = [cw_ref[j:j + 1, pl.ds(l0, LANES)] for j in range(taps)]
        bias = cb_ref[:, pl.ds(l0, LANES)]
        for r0 in range(0, tt, row_chunk):
            ext = ubuf[pl.ds(r0, CONV_HALO + row_chunk), pl.ds(l0, LANES)]
            acc = jnp.broadcast_to(bias, (row_chunk, LANES))
            for b in range(SUBLANES):
                rolled = ext if b == 0 else pltpu.roll(ext, b, axis=0)
                for a in range(CONV_HALO // SUBLANES):
                    s = SUBLANES * a + b
                    if s < taps:
                        top = CONV_HALO - SUBLANES * a
                        acc = acc + w[taps - 1 - s] * rolled[top:top + row_chunk]
            cbuf[pl.ds(r0, row_chunk), pl.ds(l0, LANES)] = acc
        return 0

    lax.fori_loop(0, c // LANES, lane_chunk, 0)

    def row_block(r, _):
        r0 = pl.multiple_of(r * row_chunk, row_chunk)
        y = cbuf[pl.ds(r0, row_chunk), :]
        yc = y - jnp.mean(y, axis=-1, keepdims=True)
        var = jnp.mean(yc * yc, axis=-1, keepdims=True)
        yn = yc * lax.rsqrt(var + NORM_EPS) * lg_ref[...] + lb_ref[...]
        o_ref[pl.ds(r0, row_chunk), :] = _silu(yn).astype(o_ref.dtype)
        return 0

    lax.fori_loop(0, tt // row_chunk, row_block, 0)


def mixer_a(proj, conv_w, conv_b, ln_g, ln_b, width, tt=256, row_chunk=64):
    m = proj.shape[0]
    taps = conv_w.shape[0]
    assert taps - 1 <= CONV_HALO
    tt = min(tt, m)
    row_chunk = min(row_chunk, tt)
    vec = lambda v: v.reshape(1, width).astype(F32)
    const = lambda i: (0, 0)
    return pl.pallas_call(
        functools.partial(_mixer_a_kernel, taps=taps, row_chunk=row_chunk),
        grid=(m // tt,),
        in_specs=[pl.BlockSpec((tt, width), lambda i: (i, 0)),
                  pl.BlockSpec((tt, width), lambda i: (i, 1)),
                  pl.BlockSpec((taps, width), const),
                  pl.BlockSpec((1, width), const),
                  pl.BlockSpec((1, width), const),
                  pl.BlockSpec((1, width), const)],
        out_specs=pl.BlockSpec((tt, width), lambda i: (i, 0)),
        out_shape=jax.ShapeDtypeStruct((m, 2 * width), BF16),
        scratch_shapes=[pltpu.VMEM((tt + CONV_HALO, width), F32),
                        pltpu.VMEM((tt, width), F32)],
        compiler_params=_params(1),
        name="mixer_a",
    )(proj, proj, conv_w.astype(F32), vec(conv_b), vec(ln_g), vec(ln_b))


def _deltanet_kernel(q_ref, k_ref, v_ref, z_ref, lg_ref, cwq_ref, cwk_ref, cwv_ref, alog_ref, dtb_ref, ng_ref,
                     mixed_hbm, o_ref, qbuf, kbuf, vbuf, state, lhs1, kn16, rhsuw, lhsb1, lhsb2, decay_s,
                     p_s, pw_s, u_s, oq_s, vnew16, *, taps, heads):
    del mixed_hbm
    i = pl.program_id(0)
    c = q_ref.shape[0]
    halo = SUBLANES
    bufs = ((q_ref, qbuf), (k_ref, kbuf), (v_ref, vbuf))

    @pl.when(i == 0)
    def _():
        state[...] = jnp.zeros(state.shape, F32)
        for _, buf in bufs:
            buf[pl.ds(0, halo), :] = jnp.zeros((halo, buf.shape[1]), F32)

    @pl.when(i > 0)
    def _():
        for _, buf in bufs:
            buf[pl.ds(0, halo), :] = buf[pl.ds(c, halo), :]

    for ref, buf in bufs:
        buf[pl.ds(halo, c), :] = ref[...]

    row = lax.broadcasted_iota(jnp.int32, (c, c), 0)
    col = lax.broadcasted_iota(jnp.int32, (c, c), 1)
    lower = row >= col
    strict = row > col
    eye = (row == col).astype(F32)

    beta_all = jax.nn.sigmoid(lg_ref[:, 0:LANES])
    g_all = -jnp.exp(alog_ref[...]) * _softplus(lg_ref[:, LANES:2 * LANES] + dtb_ref[...])
    tri = lower.astype(BF16)
    g_hi = g_all.astype(BF16)
    r1 = g_all - g_hi.astype(F32)
    g_mid = r1.astype(BF16)
    g_lo = (r1 - g_mid.astype(F32)).astype(BF16)
    gc_all = _dot(tri, g_hi) + _dot(tri, g_mid) + _dot(tri, g_lo)
    gc_t = gc_all.T
    eg_all = jnp.exp(gc_all)
    glast_row = gc_all[c - 1:c, :]
    ek_all = jnp.exp(glast_row - gc_all)
    egl_row = jnp.exp(glast_row)

    def conv_silu(buf, cw_ref, hs):
        ext = buf[pl.ds(0, halo + c), hs]
        acc = cw_ref[taps - 1:taps, hs] * ext[halo:]
        for s in range(1, taps):
            acc = acc + cw_ref[taps - 1 - s:taps - s, hs] * _shift_rows(ext, s, halo)
        return _silu(acc)

    for h in range(heads):
        hs = slice(h * HEAD_DIM, (h + 1) * HEAD_DIM)
        qh = conv_silu(qbuf, cwq_ref, hs)
        kh = conv_silu(kbuf, cwk_ref, hs)
        vh = conv_silu(vbuf, cwv_ref, hs)
        qn = qh * (lax.rsqrt(jnp.sum(qh * qh, axis=-1, keepdims=True) + NORM_EPS) * (HEAD_DIM ** -0.5))
        kn = kh * lax.rsqrt(jnp.sum(kh * kh, axis=-1, keepdims=True) + NORM_EPS)
        beta = beta_all[:, h:h + 1]
        gcol = gc_all[:, h:h + 1]
        grow = gc_t[h:h + 1, :]
        eg = eg_all[:, h:h + 1]
        kb = kn * beta
        decay_s[h] = jnp.where(lower, jnp.exp(jnp.where(lower, gcol - grow, 0.0)), 0.0)
        lhs1[h, 0:c, :] = kb.astype(BF16)
        lhs1[h, c:2 * c, :] = qn.astype(BF16)
        kn16[h] = kn.astype(BF16)
        rhsuw[h, :, 0:HEAD_DIM] = (vh * beta).astype(BF16)
        rhsuw[h, :, HEAD_DIM:2 * HEAD_DIM] = (kb * eg).astype(BF16)
        lhsb1[h, c:2 * c, :] = (qn * eg).astype(BF16)
        lhsb2[h, c:2 * c, :] = (kn * ek_all[:, h:h + 1]).T.astype(BF16)

    for h in range(heads):
        a1 = _dot_nt(lhs1[h], kn16[h])
        dec = decay_s[h]
        neg_m = jnp.where(strict, -(a1[:c] * dec), 0.0)
        p_s[h] = eye + neg_m
        pw_s[h] = neg_m
        lhsb2[h, 0:c, :] = (a1[c:] * dec).astype(BF16)

    for h in range(heads):
        n16 = pw_s[h].astype(BF16)
        pw_s[h] = _dot(n16, n16)
    span = 2
    while 2 * span < c:
        for h in range(heads):
            pw = pw_s[h]
            r = _dot(jnp.concatenate([p_s[h], pw], axis=0).astype(BF16), pw.astype(BF16))
            p_s[h] += r[:c]
            pw_s[h] = r[c:]
        span *= 2
    for h in range(heads):
        p = p_s[h]
        p_s[h] = p + _dot(p.astype(BF16), pw_s[h].astype(BF16))

    for h in range(heads):
        uw = _dot(p_s[h].astype(BF16), rhsuw[h])
        u_s[h] = uw[:, :HEAD_DIM]
        lhsb1[h, 0:c, :] = uw[:, HEAD_DIM:].astype(BF16)

    for h in range(heads):
        b1 = _dot(lhsb1[h], state[h].astype(BF16))
        vnew16[h] = (u_s[h] - b1[:c]).astype(BF16)
        oq_s[h] = b1[c:]

    for h in range(heads):
        hs = slice(h * HEAD_DIM, (h + 1) * HEAD_DIM)
        b2 = _dot(lhsb2[h], vnew16[h])
        o = oq_s[h] + b2[:c]
        state[h] = state[h] * egl_row[:, h:h + 1] + b2[c:]
        on = o * lax.rsqrt(jnp.mean(o * o, axis=-1, keepdims=True) + NORM_EPS) * ng_ref[...]
        o_ref[:, hs] = (on * _silu(z_ref[:, hs])).astype(o_ref.dtype)


def deltanet(proj, logits, conv_w, a_log, dt_bias, norm_g, mixed, width, col0):
    m = proj.shape[0]
    heads = width // HEAD_DIM
    taps = conv_w.shape[0]
    assert DN_BLOCK == HEAD_DIM and heads <= LANES and taps - 1 <= SUBLANES and m % DN_BLOCK == 0
    c = DN_BLOCK
    hd = HEAD_DIM
    pad = lambda v: jnp.pad(v.astype(F32), (0, LANES - heads)).reshape(1, LANES)
    const = lambda i: (0, 0)
    blk = lambda cb: pl.BlockSpec((c, width), lambda i, cb=cb: (i, cb))
    cw = conv_w.astype(F32)
    return pl.pallas_call(
        functools.partial(_deltanet_kernel, taps=taps, heads=heads),
        grid=(m // c,),
        in_specs=[blk(col0), blk(col0 + 1), blk(col0 + 2), blk(col0 + 3),
                  pl.BlockSpec((c, 2 * LANES), lambda i: (i, 0)),
                  pl.BlockSpec((taps, width), lambda i: (0, 0)),
                  pl.BlockSpec((taps, width), lambda i: (0, 1)),
                  pl.BlockSpec((taps, width), lambda i: (0, 2)),
                  pl.BlockSpec((1, LANES), const),
                  pl.BlockSpec((1, LANES), const),
                  pl.BlockSpec((1, hd), const),
                  pl.BlockSpec(memory_space=pl.ANY)],
        out_specs=pl.BlockSpec((c, width), lambda i: (i, 1)),
        out_shape=jax.ShapeDtypeStruct(mixed.shape, mixed.dtype),
        input_output_aliases={11: 0},
        scratch_shapes=[pltpu.VMEM((c + SUBLANES, width), F32)] * 3
                       + [pltpu.VMEM((heads, hd, hd), F32),
                          pltpu.VMEM((heads, 2 * c, hd), BF16),
                          pltpu.VMEM((heads, c, hd), BF16),
                          pltpu.VMEM((heads, c, 2 * hd), BF16),
                          pltpu.VMEM((heads, 2 * c, hd), BF16),
                          pltpu.VMEM((heads, c + hd, c), BF16),
                          pltpu.VMEM((heads, c, c), F32),
                          pltpu.VMEM((heads, c, c), F32),
                          pltpu.VMEM((heads, c, c), F32),
                          pltpu.VMEM((heads, c, hd), F32),
                          pltpu.VMEM((heads, c, hd), F32),
                          pltpu.VMEM((heads, c, hd), BF16)],
        compiler_params=_params(1),
        name="deltanet",
    )(proj, proj, proj, proj, logits, cw, cw, cw, pad(a_log), pad(dt_bias), norm_g.reshape(1, hd).astype(F32), mixed)


POOL_HALO = 16


def _odd_mix_kernel(gb_ref, gc_ref, val_ref, p_ref, cw_ref, pw_ref, ps_ref, o_ref, cvbuf, pbuf, *, taps):
    i = pl.program_id(0)
    tt, cw_width = gb_ref.shape
    pool_width = p_ref.shape[1]
    halo = SUBLANES

    @pl.when(i == 0)
    def _():
        cvbuf[pl.ds(0, halo), :] = jnp.zeros((halo, cw_width), F32)
        pbuf[pl.ds(0, POOL_HALO), :] = jnp.zeros((POOL_HALO, pool_width), F32)

    @pl.when(i > 0)
    def _():
        cvbuf[pl.ds(0, halo), :] = cvbuf[pl.ds(tt, halo), :]
        pbuf[pl.ds(0, POOL_HALO), :] = pbuf[pl.ds(tt, POOL_HALO), :]

    cvbuf[pl.ds(halo, tt), :] = gc_ref[...] * val_ref[...]
    pbuf[pl.ds(POOL_HALO, tt), :] = p_ref[...]

    conv = cw_ref[0:1, :] * cvbuf[pl.ds(halo - (taps - 1), tt), :]
    for j in range(1, taps):
        conv = conv + cw_ref[j:j + 1, :] * cvbuf[pl.ds(halo - (taps - 1) + j, tt), :]
    o_ref[:, 0:cw_width] = (gb_ref[...] * conv).astype(o_ref.dtype)

    groups = len(POOL_WINDOWS)
    gd = pool_width // groups
    t_count = (i * tt + 1 + lax.broadcasted_iota(jnp.int32, (tt, 1), 0)).astype(F32)
    for g, win in enumerate(POOL_WINDOWS):
        gs = slice(g * gd, (g + 1) * gd)
        self_rows = pbuf[pl.ds(POOL_HALO, tt), gs]
        acc = self_rows
        for back in range(1, win):
            acc = acc + pbuf[pl.ds(POOL_HALO - back, tt), gs]
        mixed = acc / jnp.minimum(t_count, float(win)) - self_rows
        d = _dot(mixed.astype(BF16), pw_ref[g]) * ps_ref[:, gs]
        o_ref[:, cw_width + g * gd:cw_width + (g + 1) * gd] = d.astype(o_ref.dtype)


def odd_mix(proj, conv_w, pool_w, pool_scale, width, tt=256):
    m = proj.shape[0]
    taps = conv_w.shape[0]
    groups, gd, _ = pool_w.shape
    assert groups == len(POOL_WINDOWS) and max(POOL_WINDOWS) - 1 <= POOL_HALO and taps - 1 <= SUBLANES
    tt = min(tt, m)
    const2 = lambda i: (0, 0)
    return pl.pallas_call(
        functools.partial(_odd_mix_kernel, taps=taps),
        grid=(m // tt,),
        in_specs=[pl.BlockSpec((tt, width), lambda i: (i, 0)),
                  pl.BlockSpec((tt, width), lambda i: (i, 1)),
                  pl.BlockSpec((tt, width), lambda i: (i, 2)),
                  pl.BlockSpec((tt, width), lambda i: (i, 3)),
                  pl.BlockSpec((taps, width), const2),
                  pl.BlockSpec((groups, gd, gd), lambda i: (0, 0, 0)),
                  pl.BlockSpec((1, width), const2)],
        out_specs=pl.BlockSpec((tt, 2 * width), lambda i: (i, 0)),
        out_shape=jax.ShapeDtypeStruct((m, 2 * width), BF16),
        scratch_shapes=[pltpu.VMEM((tt + SUBLANES, width), F32),
                        pltpu.VMEM((tt + POOL_HALO, width), F32)],
        compiler_params=_params(1),
        name="odd_mix",
    )(proj, proj, proj, proj, conv_w.astype(F32), pool_w.astype(BF16), pool_scale.reshape(1, width).astype(F32))


def kernel(x, mix_norm_g, ffn_norm_g, final_norm_g, ev_w_in, ev_w_out, a_conv_w, a_conv_b, a_ln_g, a_ln_b,
           dn_conv_w, dn_a_log, dn_dt_bias, dn_norm_g, od_w_in, od_w_out, sc_conv_w, pool_w, pool_scale,
           ffn_w_gate, ffn_conv_w, ffn_w_up, ffn_w_down):
    bsz, seq, d_model = x.shape
    assert bsz == 1, "sequence mixers carry state along the row axis; one sequence per call"
    depth = mix_norm_g.shape[0]
    half = d_model // 2
    heads = dn_a_log.shape[1]
    d_ff = ffn_w_gate.shape[2]
    ff_pad = (-d_ff) % FFN_TILE
    xs = x.reshape(seq, d_model).astype(F32)

    ev_in = ev_w_in.astype(BF16)
    w_log = ev_in[:, :, 6 * half:]
    lane_pad = ((0, 0), (0, 0), (0, LANES - heads))
    w_log = jnp.concatenate([jnp.pad(w_log[:, :, :heads], lane_pad), jnp.pad(w_log[:, :, heads:], lane_pad)], axis=2)
    ev_out = ev_w_out.astype(BF16)
    od_in = od_w_in.astype(BF16)
    od_out = od_w_out.astype(BF16)
    w_gate = jnp.pad(ffn_w_gate.astype(BF16), ((0, 0), (0, 0), (0, ff_pad)))
    w_up = jnp.pad(ffn_w_up.astype(BF16), ((0, 0), (0, 0), (0, ff_pad)))
    w_down = jnp.pad(ffn_w_down.astype(BF16), ((0, 0), (0, ff_pad), (0, 0)))
    conv_w = jnp.pad(ffn_conv_w.astype(F32), ((0, 0), (0, 0), (0, ff_pad)))

    xg, ssq = norm_inputs(xs, mix_norm_g[0])
    for layer in range(depth):
        j = layer // 2
        if layer % 2 == 0:
            proj = proj_matmul(xg, ssq, ev_in, j, 6 * half, tn=1024)
            logits = proj_matmul(xg, ssq, w_log, j, 2 * LANES, tn=2 * LANES)
            mixed = mixer_a(proj, a_conv_w[j], a_conv_b[j], a_ln_g[j], a_ln_b[j], half)
            mixed = deltanet(proj, logits, dn_conv_w[j], dn_a_log[j], dn_dt_bias[j], dn_norm_g[j], mixed, half, col0=2)
            xs, xg, ssq = out_proj(mixed, ev_out, j, xs, ffn_norm_g[layer])
        else:
            proj = proj_matmul(xg, ssq, od_in, j, 4 * half, tn=1024)
            mixed = odd_mix(proj, sc_conv_w[j], pool_w[j], pool_scale[j], half)
            xs, xg, ssq = out_proj(mixed, od_out, j, xs, ffn_norm_g[layer])
        act = ffn_act(xg, ssq, w_gate, w_up, conv_w, layer)
        g_next = mix_norm_g[layer + 1] if layer + 1 < depth else None
        xs, xg, ssq = ffn_down(act, w_down, layer, xs, g_next)
    return rmsnorm(xs, final_norm_g, x.dtype).reshape(bsz, seq, d_model)
```

```python
import functools

import jax
import jax.numpy as jnp
from jax import lax
from jax.experimental import pallas as pl
from jax.experimental.pallas import tpu as pltpu

F32 = jnp.float32
BF16 = jnp.bfloat16
NORM_EPS = 1e-6
LANES = 128
SUBLANES = 8
MXU_WIDTH = 256
HEAD_DIM = 128
DN_BLOCK = 128
POOL_WINDOWS = (2, 4, 8, 16)
VMEM_LIMIT_BYTES = 60000 * 1024
FFN_TILE = 512
ROW_TILE = 1024


def _params(n_axes):
    return pltpu.CompilerParams(dimension_semantics=("arbitrary",) * n_axes,
                                vmem_limit_bytes=VMEM_LIMIT_BYTES)


def _silu(x):
    return x * jax.nn.sigmoid(x)


def _softplus(x):
    return jnp.maximum(x, 0.0) + jnp.log1p(jnp.exp(-jnp.abs(x)))


def _dot(a, b):
    return jnp.dot(a, b, preferred_element_type=F32)


def _dot_nt(a, b):
    return lax.dot_general(a, b, (((1,), (1,)), ((), ())), preferred_element_type=F32)


def _lane_chunks(v):
    return [v[:, c0:c0 + LANES] for c0 in range(0, v.shape[1], LANES)]


def _scale_rows(v, r):
    return jnp.concatenate([ch * r for ch in _lane_chunks(v)], axis=1)


def _row_scale(ssq_ref, d_model):
    tot = jnp.sum(ssq_ref[...], axis=0)
    ms = jnp.sum(tot, axis=-1, keepdims=True) * (1.0 / d_model)
    return jnp.broadcast_to(lax.rsqrt(ms + NORM_EPS), tot.shape)


def _emit_norm_inputs(x_new, g_ref, xg_ref):
    xg_ref[...] = (x_new * g_ref[...]).astype(xg_ref.dtype)
    return functools.reduce(lambda a, b: a + b, _lane_chunks(x_new * x_new))


def _shift_rows(ext, shift, halo):
    return pltpu.roll(ext, shift, axis=0)[halo:]


def _rmsnorm_kernel(x_ref, g_ref, o_ref):
    x = x_ref[...]
    ms = jnp.mean(x * x, axis=-1, keepdims=True)
    o_ref[...] = (x * lax.rsqrt(ms + NORM_EPS) * g_ref[...]).astype(o_ref.dtype)


def rmsnorm(x, g, out_dtype, tm=256):
    m, d = x.shape
    return pl.pallas_call(
        _rmsnorm_kernel,
        grid=(m // tm,),
        in_specs=[pl.BlockSpec((tm, d), lambda i: (i, 0)),
                  pl.BlockSpec((1, d), lambda i: (0, 0))],
        out_specs=pl.BlockSpec((tm, d), lambda i: (i, 0)),
        out_shape=jax.ShapeDtypeStruct((m, d), out_dtype),
        compiler_params=_params(1),
        name="rmsnorm",
    )(x, g.reshape(1, d).astype(F32))


def _norm_inputs_kernel(x_ref, g_ref, xg_ref, ssq_ref):
    ssq_ref[...] = _emit_norm_inputs(x_ref[...], g_ref, xg_ref)


def norm_inputs(x, g, tm=256):
    m, d = x.shape
    return pl.pallas_call(
        _norm_inputs_kernel,
        grid=(m // tm,),
        in_specs=[pl.BlockSpec((tm, d), lambda i: (i, 0)),
                  pl.BlockSpec((1, d), lambda i: (0, 0))],
        out_specs=[pl.BlockSpec((tm, d), lambda i: (i, 0)),
                   pl.BlockSpec((None, tm, LANES), lambda i: (0, i, 0))],
        out_shape=[jax.ShapeDtypeStruct((m, d), BF16),
                   jax.ShapeDtypeStruct((1, m, LANES), F32)],
        compiler_params=_params(1),
        name="norm_inputs",
    )(x, g.reshape(1, d).astype(F32))


def _proj_kernel(xg_ref, ssq_ref, w_ref, o_ref, r_s):
    @pl.when(pl.program_id(1) == 0)
    def _():
        r_s[...] = _row_scale(ssq_ref, xg_ref.shape[1])

    for c0 in range(0, o_ref.shape[1], MXU_WIDTH):
        cs = slice(c0, c0 + MXU_WIDTH)
        o_ref[:, cs] = _scale_rows(_dot(xg_ref[...], w_ref[:, cs]), r_s[...]).astype(o_ref.dtype)


def proj_matmul(xg, ssq, w, n_cols, tn, tm=ROW_TILE):
    m, k = xg.shape
    tm, tn = min(tm, m), min(tn, n_cols)
    parts = ssq.shape[0]
    return pl.pallas_call(
        _proj_kernel,
        grid=(m // tm, n_cols // tn),
        in_specs=[pl.BlockSpec((tm, k), lambda i, j: (i, 0)),
                  pl.BlockSpec((parts, tm, LANES), lambda i, j: (0, i, 0)),
                  pl.BlockSpec((k, tn), lambda i, j: (0, j))],
        out_specs=pl.BlockSpec((tm, tn), lambda i, j: (i, j)),
        out_shape=jax.ShapeDtypeStruct((m, n_cols), F32),
        scratch_shapes=[pltpu.VMEM((tm, LANES), F32)],
        compiler_params=_params(2),
        name="proj_matmul",
    )(xg, ssq, w)


def _out_proj_kernel(a_ref, w_ref, x_ref, g_ref, o_ref, xg_ref, ssq_ref):
    j = pl.program_id(1)
    part = None
    for c0 in range(0, o_ref.shape[1], MXU_WIDTH):
        cs = slice(c0, c0 + MXU_WIDTH)
        x_new = x_ref[:, cs] + _dot(a_ref[...], w_ref[:, cs])
        o_ref[:, cs] = x_new
        sq = _emit_norm_inputs(x_new, g_ref.at[:, cs], xg_ref.at[:, cs])
        part = sq if part is None else part + sq

    @pl.when(j == 0)
    def _():
        ssq_ref[...] = part

    @pl.when(j > 0)
    def _():
        ssq_ref[...] += part


def out_proj(a, w, x, g_next, tm=ROW_TILE, tn=512):
    m, k = a.shape
    n = w.shape[1]
    tm, tn = min(tm, m), min(tn, n)
    return pl.pallas_call(
        _out_proj_kernel,
        grid=(m // tm, n // tn),
        in_specs=[pl.BlockSpec((tm, k), lambda i, j: (i, 0)),
                  pl.BlockSpec((k, tn), lambda i, j: (0, j)),
                  pl.BlockSpec((tm, tn), lambda i, j: (i, j)),
                  pl.BlockSpec((1, tn), lambda i, j: (0, j))],
        out_specs=[pl.BlockSpec((tm, tn), lambda i, j: (i, j)),
                   pl.BlockSpec((tm, tn), lambda i, j: (i, j)),
                   pl.BlockSpec((None, tm, LANES), lambda i, j: (0, i, 0))],
        out_shape=[jax.ShapeDtypeStruct((m, n), F32),
                   jax.ShapeDtypeStruct((m, n), BF16),
                   jax.ShapeDtypeStruct((1, m, LANES), F32)],
        compiler_params=_params(2),
        name="out_proj",
    )(a, w, x, g_next.reshape(1, n).astype(F32))


def _ffn_down_kernel(a_ref, w_ref, x_ref, g_ref, o_ref, *norm_refs):
    kk = pl.program_id(2)

    @pl.when(kk == 0)
    def _():
        o_ref[...] = x_ref[...]

    o_ref[...] += _dot(a_ref[...], w_ref[...])

    if norm_refs:
        xg_ref, ssq_ref = norm_refs

        @pl.when(kk == pl.num_programs(2) - 1)
        def _():
            ssq_ref[...] = _emit_norm_inputs(o_ref[...], g_ref, xg_ref)


def ffn_down(a, w, x, g_next, tm=ROW_TILE, tn=2048, tk=1024):
    m, k = a.shape
    n = w.shape[1]
    tm, tn, tk = min(tm, m), min(tn, n), min(tk, k)
    emit = g_next is not None
    g = (g_next if emit else jnp.ones((n,), F32)).reshape(1, n).astype(F32)
    blk = pl.BlockSpec((tm, tn), lambda j, i, kk: (i, j))
    out_specs = [blk]
    out_shape = [jax.ShapeDtypeStruct((m, n), F32)]
    if emit:
        out_specs += [blk, pl.BlockSpec((None, tm, LANES), lambda j, i, kk: (j, i, 0))]
        out_shape += [jax.ShapeDtypeStruct((m, n), BF16), jax.ShapeDtypeStruct((n // tn, m, LANES), F32)]
    res = pl.pallas_call(
        _ffn_down_kernel,
        grid=(n // tn, m // tm, k // tk),
        in_specs=[pl.BlockSpec((tm, tk), lambda j, i, kk: (i, kk)),
                  pl.BlockSpec((tk, tn), lambda j, i, kk: (kk, j)),
                  blk,
                  pl.BlockSpec((1, tn), lambda j, i, kk: (0, j))],
        out_specs=out_specs,
        out_shape=out_shape,
        compiler_params=_params(3),
        name="ffn_down",
    )(a, w, x, g)
    return res if emit else (res[0], None, None)


BF16_SUBLANES = 16


def _cast_rows_block(src_ref, dst_ref, step, n_src_blocks, n_dst_blocks):
    t = jnp.minimum(step, n_dst_blocks - 1)
    v = src_ref[...].astype(dst_ref.dtype)
    v = jnp.where(t < n_src_blocks, v, jnp.zeros_like(v))
    cols = v.shape[1]
    dst_ref[:, 0:cols] = v
    if cols < dst_ref.shape[1]:
        dst_ref[:, cols:] = jnp.zeros((v.shape[0], dst_ref.shape[1] - cols), dst_ref.dtype)


def _cast_specs(src, layer, out_rows, out_cols, rb, step_of):
    _, rows, cols = src.shape
    assert rows % rb == 0 and out_rows % rb == 0 and rb % BF16_SUBLANES == 0 and cols <= out_cols
    n_src, n_out = rows // rb, out_rows // rb
    in_spec = pl.BlockSpec((None, rb, cols), lambda *g: (layer, jnp.minimum(step_of(*g), n_src - 1), 0))
    out_spec = pl.BlockSpec((rb, out_cols), lambda *g: (jnp.minimum(step_of(*g), n_out - 1), 0))
    return in_spec, out_spec, n_src, n_out


def _cast_row_block(rows, out_rows, max_blocks=None):
    sizes = [r for r in (16, 32, 64, 128, 256) if rows % r == 0 and out_rows % r == 0]
    if max_blocks is None:
        return sizes[-1]
    fits = [r for r in sizes if out_rows // r <= max_blocks]
    assert fits, "cast does not fit the host grid"
    return fits[0]


def _cast_kernel(src_ref, dst_ref, *, n_src):
    _cast_rows_block(src_ref, dst_ref, pl.program_id(0), n_src, pl.num_programs(0))


def cast_weight(w, layer, out_rows=None, out_cols=None):
    _, rows, cols = w.shape
    out_rows, out_cols = out_rows or rows, out_cols or cols
    rb = _cast_row_block(rows, out_rows)
    in_spec, out_spec, n_src, n_out = _cast_specs(w, layer, out_rows, out_cols, rb, lambda i: i)
    return pl.pallas_call(
        functools.partial(_cast_kernel, n_src=n_src),
        grid=(n_out,),
        in_specs=[in_spec],
        out_specs=out_spec,
        out_shape=jax.ShapeDtypeStruct((out_rows, out_cols), BF16),
        compiler_params=_params(1),
        name="cast_weight",
    )(w)


def _ffn_act_kernel(xg_ref, ssq_ref, wg_ref, wu_ref, cw_ref, *rest, sub, cast_blocks):
    n_cast = len(cast_blocks)
    cast_src = rest[:n_cast]
    o_ref = rest[n_cast]
    cast_dst = rest[n_cast + 1:2 * n_cast + 1]
    carry, r_s = rest[2 * n_cast + 1:]
    i, j = pl.program_id(0), pl.program_id(1)
    tm = xg_ref.shape[0]
    tf = o_ref.shape[1]
    taps = cw_ref.shape[0]

    @pl.when(j == 0)
    def _():
        r_s[...] = _row_scale(ssq_ref, xg_ref.shape[1])

    @pl.when(i == 0)
    def _():
        carry[j] = jnp.zeros(carry.shape[1:], F32)

    xg = xg_ref[...]
    r = r_s[...]
    for c0 in range(0, tf, sub):
        cs = slice(c0, c0 + sub)
        gate = _scale_rows(_dot(xg, wg_ref[:, cs]), r)
        up = _scale_rows(_dot(xg, wu_ref[:, cs]), r)
        ext = jnp.concatenate([carry[j, :, cs], gate], axis=0)
        carry[j, :, cs] = gate[tm - SUBLANES:, :]
        conv = cw_ref[taps - 1:taps, cs] * gate
        for s in range(1, taps):
            conv = conv + cw_ref[taps - 1 - s:taps - s, cs] * _shift_rows(ext, s, SUBLANES)
        o_ref[:, cs] = (_silu(conv) * up).astype(o_ref.dtype)

    step = i * pl.num_programs(1) + j
    for src, dst, (n_src, n_dst) in zip(cast_src, cast_dst, cast_blocks):
        _cast_rows_block(src, dst, step, n_src, n_dst)


def ffn_act(xg, ssq, w_gate, w_up, conv_w, casts=(), tm=ROW_TILE, tf=FFN_TILE):
    m, d = xg.shape
    f = w_gate.shape[1]
    taps = conv_w.shape[0]
    tm = min(tm, m)
    n_i, n_j = m // tm, f // tf
    parts = ssq.shape[0]
    assert taps - 1 <= SUBLANES
    in_specs = [pl.BlockSpec((tm, d), lambda i, j: (i, 0)),
                pl.BlockSpec((parts, tm, LANES), lambda i, j: (0, i, 0)),
                pl.BlockSpec((d, tf), lambda i, j: (0, j)),
                pl.BlockSpec((d, tf), lambda i, j: (0, j)),
                pl.BlockSpec((taps, tf), lambda i, j: (0, j))]
    out_specs = [pl.BlockSpec((tm, tf), lambda i, j: (i, j))]
    out_shape = [jax.ShapeDtypeStruct((m, f), BF16)]
    args = [xg, ssq, w_gate, w_up, conv_w]
    cast_in, cast_blocks = [], []
    for src, lyr, out_rows, out_cols in casts:
        rb = _cast_row_block(src.shape[1], out_rows, max_blocks=n_i * n_j)
        in_spec, out_spec, n_src, n_dst = _cast_specs(src, lyr, out_rows, out_cols, rb, lambda i, j: i * n_j + j)
        cast_in.append(in_spec)
        cast_blocks.append((n_src, n_dst))
        out_specs.append(out_spec)
        out_shape.append(jax.ShapeDtypeStruct((out_rows, out_cols), BF16))
        args.append(src)
    in_specs += cast_in
    res = pl.pallas_call(
        functools.partial(_ffn_act_kernel, sub=MXU_WIDTH, cast_blocks=tuple(cast_blocks)),
        grid=(n_i, n_j),
        in_specs=in_specs,
        out_specs=out_specs,
        out_shape=out_shape,
        scratch_shapes=[pltpu.VMEM((n_j, SUBLANES, tf), F32),
                        pltpu.VMEM((tm, LANES), F32)],
        compiler_params=_params(2),
        name="ffn_act",
    )(*args)
    return res[0], tuple(res[1:])


CONV_HALO = 32


def _mixer_a_kernel(av_ref, ag_ref, cw_ref, cb_ref, lg_ref, lb_ref, o_ref, ubuf, cbuf, *, taps, row_chunk):
    i = pl.program_id(0)
    tt, c = av_ref.shape

    @pl.when(i == 0)
    def _():
        ubuf[pl.ds(0, CONV_HALO), :] = jnp.zeros((CONV_HALO, c), F32)

    @pl.when(i > 0)
    def _():
        ubuf[pl.ds(0, CONV_HALO), :] = ubuf[pl.ds(tt, CONV_HALO), :]

    ubuf[pl.ds(CONV_HALO, tt), :] = av_ref[...] * jax.nn.sigmoid(ag_ref[...])

    def lane_chunk(l, _):
        l0 = pl.multiple_of(l * LANES, LANES)
        w = [cw_ref[j:j + 1, pl.ds(l0, LANES)] for j in range(taps)]
        bias = cb_ref[:, pl.ds(l0, LANES)]
        for r0 in range(0, tt, row_chunk):
            ext = ubuf[pl.ds(r0, CONV_HALO + row_chunk), pl.ds(l0, LANES)]
            acc = jnp.broadcast_to(bias, (row_chunk, LANES))
            for b in range(SUBLANES):
                rolled = ext if b == 0 else pltpu.roll(ext, b, axis=0)
                for a in range(CONV_HALO // SUBLANES):
                    s = SUBLANES * a + b
                    if s < taps:
                        top = CONV_HALO - SUBLANES * a
                        acc = acc + w[taps - 1 - s] * rolled[top:top + row_chunk]
            cbuf[pl.ds(r0, row_chunk), pl.ds(l0, LANES)] = acc
        return 0

    lax.fori_loop(0, c // LANES, lane_chunk, 0)

    def row_block(r, _):
        r0 = pl.multiple_of(r * row_chunk, row_chunk)
        y = cbuf[pl.ds(r0, row_chunk), :]
        yc = y - jnp.mean(y, axis=-1, keepdims=True)
        var = jnp.mean(yc * yc, axis=-1, keepdims=True)
        yn = yc * lax.rsqrt(var + NORM_EPS) * lg_ref[...] + lb_ref[...]
        o_ref[pl.ds(r0, row_chunk), :] = _silu(yn).astype(o_ref.dtype)
        return 0

    lax.fori_loop(0, tt // row_chunk, row_block, 0)


def mixer_a(proj, conv_w, conv_b, ln_g, ln_b, width, tt=256, row_chunk=64):
    m = proj.shape[0]
    taps = conv_w.shape[0]
    assert taps - 1 <= CONV_HALO
    tt = min(tt, m)
    row_chunk = min(row_chunk, tt)
    vec = lambda v: v.reshape(1, width).astype(F32)
    const = lambda i: (0, 0)
    return pl.pallas_call(
        functools.partial(_mixer_a_kernel, taps=taps, row_chunk=row_chunk),
        grid=(m // tt,),
        in_specs=[pl.BlockSpec((tt, width), lambda i: (i, 0)),
                  pl.BlockSpec((tt, width), lambda i: (i, 1)),
                  pl.BlockSpec((taps, width), const),
                  pl.BlockSpec((1, width), const),
                  pl.BlockSpec((1, width), const),
                  pl.BlockSpec((1, width), const)],
        out_specs=pl.BlockSpec((tt, width), lambda i: (i, 0)),
        out_shape=jax.ShapeDtypeStruct((m, 2 * width), BF16),
        scratch_shapes=[pltpu.VMEM((tt + CONV_HALO, width), F32),
                        pltpu.VMEM((tt, width), F32)],
        compiler_params=_params(1),
        name="mixer_a",
    )(proj, proj, conv_w.astype(F32), vec(conv_b), vec(ln_g), vec(ln_b))


def _deltanet_kernel(q_ref, k_ref, v_ref, z_ref, lg_ref, cwq_ref, cwk_ref, cwv_ref, alog_ref, dtb_ref, ng_ref,
                     mixed_hbm, o_ref, qbuf, kbuf, vbuf, state, lhs1, kn16, rhsuw, lhsb1, lhsb2, decay_s,
                     p_s, pw_s, u_s, oq_s, vnew16, *, taps, heads, head_groups):
    del mixed_hbm
    i = pl.program_id(0)
    c = q_ref.shape[0]
    halo = SUBLANES
    bufs = ((q_ref, qbuf), (k_ref, kbuf), (v_ref, vbuf))

    @pl.when(i == 0)
    def _():
        state[...] = jnp.zeros(state.shape, F32)
        for _, buf in bufs:
            buf[pl.ds(0, halo), :] = jnp.zeros((halo, buf.shape[1]), F32)

    @pl.when(i > 0)
    def _():
        for _, buf in bufs:
            buf[pl.ds(0, halo), :] = buf[pl.ds(c, halo), :]

    for ref, buf in bufs:
        buf[pl.ds(halo, c), :] = ref[...]

    row = lax.broadcasted_iota(jnp.int32, (c, c), 0)
    col = lax.broadcasted_iota(jnp.int32, (c, c), 1)
    lower = row >= col
    strict = row > col
    eye = (row == col).astype(F32)

    beta_all = jax.nn.sigmoid(lg_ref[:, 0:LANES])
    g_all = -jnp.exp(alog_ref[...]) * _softplus(lg_ref[:, LANES:2 * LANES] + dtb_ref[...])
    tri = lower.astype(BF16)
    g_hi = g_all.astype(BF16)
    r1 = g_all - g_hi.astype(F32)
    g_mid = r1.astype(BF16)
    g_lo = (r1 - g_mid.astype(F32)).astype(BF16)
    gc_all = _dot(tri, g_hi) + _dot(tri, g_mid) + _dot(tri, g_lo)
    gc_t = gc_all.T
    eg_all = jnp.exp(gc_all)
    glast_row = gc_all[c - 1:c, :]
    ek_all = jnp.exp(glast_row - gc_all)
    egl_row = jnp.exp(glast_row)

    def conv_silu(buf, cw_ref, hs):
        ext = buf[pl.ds(0, halo + c), hs]
        acc = cw_ref[taps - 1:taps, hs] * ext[halo:]
        for s in range(1, taps):
            acc = acc + cw_ref[taps - 1 - s:taps - s, hs] * _shift_rows(ext, s, halo)
        return _silu(acc)

    def prepare(h):
        hs = slice(h * HEAD_DIM, (h + 1) * HEAD_DIM)
        qh = conv_silu(qbuf, cwq_ref, hs)
        kh = conv_silu(kbuf, cwk_ref, hs)
        vh = conv_silu(vbuf, cwv_ref, hs)
        qn = qh * (lax.rsqrt(jnp.sum(qh * qh, axis=-1, keepdims=True) + NORM_EPS) * (HEAD_DIM ** -0.5))
        kn = kh * lax.rsqrt(jnp.sum(kh * kh, axis=-1, keepdims=True) + NORM_EPS)
        beta = beta_all[:, h:h + 1]
        gcol = gc_all[:, h:h + 1]
        grow = gc_t[h:h + 1, :]
        eg = eg_all[:, h:h + 1]
        kb = kn * beta
        decay_s[h] = jnp.where(lower, jnp.exp(jnp.where(lower, gcol - grow, 0.0)), 0.0)
        lhs1[h, 0:c, :] = kb.astype(BF16)
        lhs1[h, c:2 * c, :] = qn.astype(BF16)
        kn16[h] = kn.astype(BF16)
        rhsuw[h, :, 0:HEAD_DIM] = (vh * beta).astype(BF16)
        rhsuw[h, :, HEAD_DIM:2 * HEAD_DIM] = (kb * eg).astype(BF16)
        lhsb1[h, c:2 * c, :] = (qn * eg).astype(BF16)
        lhsb2[h, c:2 * c, :] = (kn * ek_all[:, h:h + 1]).T.astype(BF16)

    def matmul_stages(hl):
        for h in hl:
            a1 = _dot_nt(lhs1[h], kn16[h])
            dec = decay_s[h]
            neg_m = jnp.where(strict, -(a1[:c] * dec), 0.0)
            p_s[h] = eye + neg_m
            pw_s[h] = neg_m
            lhsb2[h, 0:c, :] = (a1[c:] * dec).astype(BF16)

        for h in hl:
            n16 = pw_s[h].astype(BF16)
            pw_s[h] = _dot(n16, n16)
        span = 2
        while 2 * span < c:
            for h in hl:
                pw = pw_s[h]
                r = _dot(jnp.concatenate([p_s[h], pw], axis=0).astype(BF16), pw.astype(BF16))
                p_s[h] += r[:c]
                pw_s[h] = r[c:]
            span *= 2
        for h in hl:
            p = p_s[h]
            p_s[h] = p + _dot(p.astype(BF16), pw_s[h].astype(BF16))

        for h in hl:
            uw = _dot(p_s[h].astype(BF16), rhsuw[h])
            u_s[h] = uw[:, :HEAD_DIM]
            lhsb1[h, 0:c, :] = uw[:, HEAD_DIM:].astype(BF16)

        for h in hl:
            b1 = _dot(lhsb1[h], state[h].astype(BF16))
            vnew16[h] = (u_s[h] - b1[:c]).astype(BF16)
            oq_s[h] = b1[c:]

        for h in hl:
            hs = slice(h * HEAD_DIM, (h + 1) * HEAD_DIM)
            b2 = _dot(lhsb2[h], vnew16[h])
            o = oq_s[h] + b2[:c]
            state[h] = state[h] * egl_row[:, h:h + 1] + b2[c:]
            on = o * lax.rsqrt(jnp.mean(o * o, axis=-1, keepdims=True) + NORM_EPS) * ng_ref[...]
            o_ref[:, hs] = (on * _silu(z_ref[:, hs])).astype(o_ref.dtype)

    per = heads // head_groups
    groups = [list(range(g * per, (g + 1) * per)) for g in range(head_groups)]
    for h in groups[0]:
        prepare(h)
    for g in range(head_groups):
        if g + 1 < head_groups:
            for h in groups[g + 1]:
                prepare(h)
        matmul_stages(groups[g])


def deltanet(proj, logits, conv_w, a_log, dt_bias, norm_g, mixed, width, col0):
    m = proj.shape[0]
    heads = width // HEAD_DIM
    taps = conv_w.shape[0]
    assert DN_BLOCK == HEAD_DIM and heads <= LANES and taps - 1 <= SUBLANES and m % DN_BLOCK == 0
    c = DN_BLOCK
    hd = HEAD_DIM
    head_groups = 2 if heads % 2 == 0 else 1
    pad = lambda v: jnp.pad(v.astype(F32), (0, LANES - heads)).reshape(1, LANES)
    const = lambda i: (0, 0)
    blk = lambda cb: pl.BlockSpec((c, width), lambda i, cb=cb: (i, cb))
    cw = conv_w.astype(F32)
    return pl.pallas_call(
        functools.partial(_deltanet_kernel, taps=taps, heads=heads, head_groups=head_groups),
        grid=(m // c,),
        in_specs=[blk(col0), blk(col0 + 1), blk(col0 + 2), blk(col0 + 3),
                  pl.BlockSpec((c, 2 * LANES), lambda i: (i, 0)),
                  pl.BlockSpec((taps, width), lambda i: (0, 0)),
                  pl.BlockSpec((taps, width), lambda i: (0, 1)),
                  pl.BlockSpec((taps, width), lambda i: (0, 2)),
                  pl.BlockSpec((1, LANES), const),
                  pl.BlockSpec((1, LANES), const),
                  pl.BlockSpec((1, hd), const),
                  pl.BlockSpec(memory_space=pl.ANY)],
        out_specs=pl.BlockSpec((c, width), lambda i: (i, 1)),
        out_shape=jax.ShapeDtypeStruct(mixed.shape, mixed.dtype),
        input_output_aliases={11: 0},
        scratch_shapes=[pltpu.VMEM((c + SUBLANES, width), F32)] * 3
                       + [pltpu.VMEM((heads, hd, hd), F32),
                        ] + [
                          pltpu.VMEM((heads, 2 * c, hd), BF16),
                          pltpu.VMEM((heads, c, hd), BF16),
                          pltpu.VMEM((heads, c, 2 * hd), BF16),
                          pltpu.VMEM((heads, 2 * c, hd), BF16),
                          pltpu.VMEM((heads, c + hd, c), BF16),
                          pltpu.VMEM((heads, c, c), F32),
                          pltpu.VMEM((heads, c, c), F32),
                          pltpu.VMEM((heads, c, c), F32),
                          pltpu.VMEM((heads, c, hd), F32),
                          pltpu.VMEM((heads, c, hd), F32),
                          pltpu.VMEM((heads, c, hd), BF16)],
        compiler_params=_params(1),
        name="deltanet",
    )(proj, proj, proj, proj, logits, cw, cw, cw, pad(a_log), pad(dt_bias), norm_g.reshape(1, hd).astype(F32), mixed)


POOL_HALO = 16


def _odd_mix_kernel(gb_ref, gc_ref, val_ref, p_ref, cw_ref, pw_ref, ps_ref, o_ref, cvbuf, pbuf, *, taps):
    i = pl.program_id(0)
    tt, cw_width = gb_ref.shape
    pool_width = p_ref.shape[1]
    halo = SUBLANES

    @pl.when(i == 0)
    def _():
        cvbuf[pl.ds(0, halo), :] = jnp.zeros((halo, cw_width), F32)
        pbuf[pl.ds(0, POOL_HALO), :] = jnp.zeros((POOL_HALO, pool_width), F32)

    @pl.when(i > 0)
    def _():
        cvbuf[pl.ds(0, halo), :] = cvbuf[pl.ds(tt, halo), :]
        pbuf[pl.ds(0, POOL_HALO), :] = pbuf[pl.ds(tt, POOL_HALO), :]

    cvbuf[pl.ds(halo, tt), :] = gc_ref[...] * val_ref[...]
    pbuf[pl.ds(POOL_HALO, tt), :] = p_ref[...]

    conv = cw_ref[0:1, :] * cvbuf[pl.ds(halo - (taps - 1), tt), :]
    for j in range(1, taps):
        conv = conv + cw_ref[j:j + 1, :] * cvbuf[pl.ds(halo - (taps - 1) + j, tt), :]
    o_ref[:, 0:cw_width] = (gb_ref[...] * conv).astype(o_ref.dtype)

    groups = len(POOL_WINDOWS)
    gd = pool_width // groups
    t_count = (i * tt + 1 + lax.broadcasted_iota(jnp.int32, (tt, 1), 0)).astype(F32)
    for g, win in enumerate(POOL_WINDOWS):
        gs = slice(g * gd, (g + 1) * gd)
        self_rows = pbuf[pl.ds(POOL_HALO, tt), gs]
        acc = self_rows
        for back in range(1, win):
            acc = acc + pbuf[pl.ds(POOL_HALO - back, tt), gs]
        mixed = acc / jnp.minimum(t_count, float(win)) - self_rows
        d = _dot(mixed.astype(BF16), pw_ref[g]) * ps_ref[:, gs]
        o_ref[:, cw_width + g * gd:cw_width + (g + 1) * gd] = d.astype(o_ref.dtype)


def odd_mix(proj, conv_w, pool_w, pool_scale, width, tt=256):
    m = proj.shape[0]
    taps = conv_w.shape[0]
    groups, gd, _ = pool_w.shape
    assert groups == len(POOL_WINDOWS) and max(POOL_WINDOWS) - 1 <= POOL_HALO and taps - 1 <= SUBLANES
    tt = min(tt, m)
    const2 = lambda i: (0, 0)
    return pl.pallas_call(
        functools.partial(_odd_mix_kernel, taps=taps),
        grid=(m // tt,),
        in_specs=[pl.BlockSpec((tt, width), lambda i: (i, 0)),
                  pl.BlockSpec((tt, width), lambda i: (i, 1)),
                  pl.BlockSpec((tt, width), lambda i: (i, 2)),
                  pl.BlockSpec((tt, width), lambda i: (i, 3)),
                  pl.BlockSpec((taps, width), const2),
                  pl.BlockSpec((groups, gd, gd), lambda i: (0, 0, 0)),
                  pl.BlockSpec((1, width), const2)],
        out_specs=pl.BlockSpec((tt, 2 * width), lambda i: (i, 0)),
        out_shape=jax.ShapeDtypeStruct((m, 2 * width), BF16),
        scratch_shapes=[pltpu.VMEM((tt + SUBLANES, width), F32),
                        pltpu.VMEM((tt + POOL_HALO, width), F32)],
        compiler_params=_params(1),
        name="odd_mix",
    )(proj, proj, proj, proj, conv_w.astype(F32), pool_w.astype(BF16), pool_scale.reshape(1, width).astype(F32))


def kernel(x, mix_norm_g, ffn_norm_g, final_norm_g, ev_w_in, ev_w_out, a_conv_w, a_conv_b, a_ln_g, a_ln_b,
           dn_conv_w, dn_a_log, dn_dt_bias, dn_norm_g, od_w_in, od_w_out, sc_conv_w, pool_w, pool_scale,
           ffn_w_gate, ffn_conv_w, ffn_w_up, ffn_w_down):
    bsz, seq, d_model = x.shape
    assert bsz == 1, "sequence mixers carry state along the row axis; one sequence per call"
    depth = mix_norm_g.shape[0]
    half = d_model // 2
    heads = dn_a_log.shape[1]
    d_ff = ffn_w_gate.shape[2]
    ff_padded = d_ff + (-d_ff) % FFN_TILE
    xs = x.reshape(seq, d_model).astype(F32)

    conv_w = jnp.pad(ffn_conv_w.astype(F32), ((0, 0), (0, 0), (0, ff_padded - d_ff)))
    mix_w = (cast_weight(ev_w_in, 0), cast_weight(ev_w_out, 0))
    ffn_w = (cast_weight(ffn_w_gate, 0, out_cols=ff_padded), cast_weight(ffn_w_up, 0, out_cols=ff_padded),
             cast_weight(ffn_w_down, 0, out_rows=ff_padded))

    xg, ssq = norm_inputs(xs, mix_norm_g[0])
    for layer in range(depth):
        j = layer // 2
        w_in, w_out = mix_w
        if layer % 2 == 0:
            w_log = w_in[:, 6 * half:]
            lane_pad = ((0, 0), (0, LANES - heads))
            w_log = jnp.concatenate([jnp.pad(w_log[:, :heads], lane_pad), jnp.pad(w_log[:, heads:], lane_pad)], axis=1)
            proj = proj_matmul(xg, ssq, w_in, 6 * half, tn=1024)
            logits = proj_matmul(xg, ssq, w_log, 2 * LANES, tn=2 * LANES)
            mixed = mixer_a(proj, a_conv_w[j], a_conv_b[j], a_ln_g[j], a_ln_b[j], half)
            mixed = deltanet(proj, logits, dn_conv_w[j], dn_a_log[j], dn_dt_bias[j], dn_norm_g[j], mixed, half, col0=2)
        else:
            proj = proj_matmul(xg, ssq, w_in, 4 * half, tn=1024)
            mixed = odd_mix(proj, sc_conv_w[j], pool_w[j], pool_scale[j], half)
        xs, xg, ssq = out_proj(mixed, w_out, xs, ffn_norm_g[layer])
        if layer + 1 == depth:
            act, _ = ffn_act(xg, ssq, ffn_w[0], ffn_w[1], conv_w[layer])
            xs, _, _ = ffn_down(act, ffn_w[2], xs, None)
        else:
            nxt = layer + 1
            nxt_in, nxt_out = (ev_w_in, ev_w_out) if nxt % 2 == 0 else (od_w_in, od_w_out)
            casts = ((ffn_w_gate, nxt, d_model, ff_padded), (ffn_w_up, nxt, d_model, ff_padded),
                     (ffn_w_down, nxt, ff_padded, d_model),
                     (nxt_in, nxt // 2) + nxt_in.shape[1:], (nxt_out, nxt // 2) + nxt_out.shape[1:])
            act, cast = ffn_act(xg, ssq, ffn_w[0], ffn_w[1], conv_w[layer], casts)
            xs, xg, ssq = ffn_down(act, ffn_w[2], xs, mix_norm_g[nxt])
            ffn_w, mix_w = cast[:3], cast[3:]
    return rmsnorm(xs, final_norm_g, x.dtype).reshape(bsz, seq, d_model)
```

```python
import functools

import jax
import jax.numpy as jnp
from jax import lax
from jax.experimental import pallas as pl
from jax.experimental.pallas import tpu as pltpu

F32 = jnp.float32
BF16 = jnp.bfloat16
NORM_EPS = 1e-6
LANES = 128
SUBLANES = 8
MXU_WIDTH = 256
HEAD_DIM = 128
DN_BLOCK = 128
POOL_WINDOWS = (2, 4, 8, 16)
VMEM_LIMIT_BYTES = 60000 * 1024
FFN_TILE = 512
ROW_TILE = 1024


def _params(n_axes):
    return pltpu.CompilerParams(dimension_semantics=("arbitrary",) * n_axes,
                                vmem_limit_bytes=VMEM_LIMIT_BYTES)


def _silu(x):
    return x * jax.nn.sigmoid(x)


def _softplus(x):
    return jnp.maximum(x, 0.0) + jnp.log1p(jnp.exp(-jnp.abs(x)))


def _dot(a, b):
    return jnp.dot(a, b, preferred_element_type=F32)


def _dot_nt(a, b):
    return lax.dot_general(a, b, (((1,), (1,)), ((), ())), preferred_element_type=F32)


def _split_bf16(x):
    hi = x.astype(BF16)
    return hi, (x - hi.astype(F32)).astype(BF16)


def _lane_chunks(v):
    return [v[:, c0:c0 + LANES] for c0 in range(0, v.shape[1], LANES)]


def _scale_rows(v, r):
    return jnp.concatenate([ch * r for ch in _lane_chunks(v)], axis=1)


def _row_scale(ssq_ref, d_model):
    tot = jnp.sum(ssq_ref[...], axis=0)
    ms = jnp.sum(tot, axis=-1, keepdims=True) * (1.0 / d_model)
    return jnp.broadcast_to(lax.rsqrt(ms + NORM_EPS), tot.shape)


def _emit_norm_inputs(x_new, g_ref, xg_ref):
    xg_ref[...] = (x_new * g_ref[...]).astype(xg_ref.dtype)
    return functools.reduce(lambda a, b: a + b, _lane_chunks(x_new * x_new))


def _shift_rows(ext, shift, halo):
    return pltpu.roll(ext, shift, axis=0)[halo:]


def _rmsnorm_kernel(x_ref, g_ref, o_ref):
    x = x_ref[...]
    ms = jnp.mean(x * x, axis=-1, keepdims=True)
    o_ref[...] = (x * lax.rsqrt(ms + NORM_EPS) * g_ref[...]).astype(o_ref.dtype)


def rmsnorm(x, g, out_dtype, tm=256):
    m, d = x.shape
    return pl.pallas_call(
        _rmsnorm_kernel,
        grid=(m // tm,),
        in_specs=[pl.BlockSpec((tm, d), lambda i: (i, 0)),
                  pl.BlockSpec((1, d), lambda i: (0, 0))],
        out_specs=pl.BlockSpec((tm, d), lambda i: (i, 0)),
        out_shape=jax.ShapeDtypeStruct((m, d), out_dtype),
        compiler_params=_params(1),
        name="rmsnorm",
    )(x, g.reshape(1, d).astype(F32))


def _norm_inputs_kernel(x_ref, g_ref, xg_ref, ssq_ref):
    ssq_ref[...] = _emit_norm_inputs(x_ref[...], g_ref, xg_ref)


def norm_inputs(x, g, tm=256):
    m, d = x.shape
    return pl.pallas_call(
        _norm_inputs_kernel,
        grid=(m // tm,),
        in_specs=[pl.BlockSpec((tm, d), lambda i: (i, 0)),
                  pl.BlockSpec((1, d), lambda i: (0, 0))],
        out_specs=[pl.BlockSpec((tm, d), lambda i: (i, 0)),
                   pl.BlockSpec((None, tm, LANES), lambda i: (0, i, 0))],
        out_shape=[jax.ShapeDtypeStruct((m, d), BF16),
                   jax.ShapeDtypeStruct((1, m, LANES), F32)],
        compiler_params=_params(1),
        name="norm_inputs",
    )(x, g.reshape(1, d).astype(F32))


def _proj_kernel(xg_ref, ssq_ref, w_ref, o_ref, r_s):
    @pl.when(pl.program_id(1) == 0)
    def _():
        r_s[...] = _row_scale(ssq_ref, xg_ref.shape[1])

    for c0 in range(0, o_ref.shape[1], MXU_WIDTH):
        cs = slice(c0, c0 + MXU_WIDTH)
        o_ref[:, cs] = _scale_rows(_dot(xg_ref[...], w_ref[:, cs]), r_s[...]).astype(o_ref.dtype)


def proj_matmul(xg, ssq, w, n_cols, tn, layer=None, tm=ROW_TILE):
    m, k = xg.shape
    tm, tn = min(tm, m), min(tn, n_cols)
    parts = ssq.shape[0]
    if layer is None:
        w_spec = pl.BlockSpec((k, tn), lambda i, j: (0, j))
    else:
        w_spec = pl.BlockSpec((None, k, tn), lambda i, j: (layer, 0, j))
    return pl.pallas_call(
        _proj_kernel,
        grid=(m // tm, n_cols // tn),
        in_specs=[pl.BlockSpec((tm, k), lambda i, j: (i, 0)),
                  pl.BlockSpec((parts, tm, LANES), lambda i, j: (0, i, 0)),
                  w_spec],
        out_specs=pl.BlockSpec((tm, tn), lambda i, j: (i, j)),
        out_shape=jax.ShapeDtypeStruct((m, n_cols), F32),
        scratch_shapes=[pltpu.VMEM((tm, LANES), F32)],
        compiler_params=_params(2),
        name="proj_matmul",
    )(xg, ssq, w)


def _out_proj_kernel(a_ref, w_ref, x_ref, g_ref, o_ref, xg_ref, ssq_ref):
    j = pl.program_id(1)
    part = None
    for c0 in range(0, o_ref.shape[1], MXU_WIDTH):
        cs = slice(c0, c0 + MXU_WIDTH)
        x_new = x_ref[:, cs] + _dot(a_ref[...], w_ref[:, cs])
        o_ref[:, cs] = x_new
        sq = _emit_norm_inputs(x_new, g_ref.at[:, cs], xg_ref.at[:, cs])
        part = sq if part is None else part + sq

    @pl.when(j == 0)
    def _():
        ssq_ref[...] = part

    @pl.when(j > 0)
    def _():
        ssq_ref[...] += part


def out_proj(a, w, x, g_next, tm=ROW_TILE, tn=512):
    m, k = a.shape
    n = w.shape[1]
    tm, tn = min(tm, m), min(tn, n)
    return pl.pallas_call(
        _out_proj_kernel,
        grid=(m // tm, n // tn),
        in_specs=[pl.BlockSpec((tm, k), lambda i, j: (i, 0)),
                  pl.BlockSpec((k, tn), lambda i, j: (0, j)),
                  pl.BlockSpec((tm, tn), lambda i, j: (i, j)),
                  pl.BlockSpec((1, tn), lambda i, j: (0, j))],
        out_specs=[pl.BlockSpec((tm, tn), lambda i, j: (i, j)),
                   pl.BlockSpec((tm, tn), lambda i, j: (i, j)),
                   pl.BlockSpec((None, tm, LANES), lambda i, j: (0, i, 0))],
        out_shape=[jax.ShapeDtypeStruct((m, n), F32),
                   jax.ShapeDtypeStruct((m, n), BF16),
                   jax.ShapeDtypeStruct((1, m, LANES), F32)],
        compiler_params=_params(2),
        name="out_proj",
    )(a, w, x, g_next.reshape(1, n).astype(F32))


def _ffn_down_kernel(a_ref, w_ref, x_ref, g_ref, o_ref, *norm_refs):
    kk = pl.program_id(2)

    @pl.when(kk == 0)
    def _():
        o_ref[...] = x_ref[...]

    o_ref[...] += _dot(a_ref[...], w_ref[...])

    if norm_refs:
        xg_ref, ssq_ref = norm_refs

        @pl.when(kk == pl.num_programs(2) - 1)
        def _():
            ssq_ref[...] = _emit_norm_inputs(o_ref[...], g_ref, xg_ref)


def ffn_down(a, w, x, g_next, tm=ROW_TILE, tn=2048, tk=1024):
    m, k = a.shape
    n = w.shape[1]
    tm, tn, tk = min(tm, m), min(tn, n), min(tk, k)
    emit = g_next is not None
    g = (g_next if emit else jnp.ones((n,), F32)).reshape(1, n).astype(F32)
    blk = pl.BlockSpec((tm, tn), lambda j, i, kk: (i, j))
    out_specs = [blk]
    out_shape = [jax.ShapeDtypeStruct((m, n), F32)]
    if emit:
        out_specs += [blk, pl.BlockSpec((None, tm, LANES), lambda j, i, kk: (j, i, 0))]
        out_shape += [jax.ShapeDtypeStruct((m, n), BF16), jax.ShapeDtypeStruct((n // tn, m, LANES), F32)]
    res = pl.pallas_call(
        _ffn_down_kernel,
        grid=(n // tn, m // tm, k // tk),
        in_specs=[pl.BlockSpec((tm, tk), lambda j, i, kk: (i, kk)),
                  pl.BlockSpec((tk, tn), lambda j, i, kk: (kk, j)),
                  blk,
                  pl.BlockSpec((1, tn), lambda j, i, kk: (0, j))],
        out_specs=out_specs,
        out_shape=out_shape,
        compiler_params=_params(3),
        name="ffn_down",
    )(a, w, x, g)
    return res if emit else (res[0], None, None)


BF16_SUBLANES = 16


def _cast_rows_block(src_ref, dst_ref, step, n_src_blocks, n_dst_blocks):
    t = jnp.minimum(step, n_dst_blocks - 1)
    v = src_ref[...].astype(dst_ref.dtype)
    v = jnp.where(t < n_src_blocks, v, jnp.zeros_like(v))
    cols = v.shape[1]
    dst_ref[:, 0:cols] = v
    if cols < dst_ref.shape[1]:
        dst_ref[:, cols:] = jnp.zeros((v.shape[0], dst_ref.shape[1] - cols), dst_ref.dtype)


def _cast_specs(src, layer, out_rows, out_cols, rb, step_of):
    _, rows, cols = src.shape
    assert rows % rb == 0 and out_rows % rb == 0 and rb % BF16_SUBLANES == 0 and cols <= out_cols
    n_src, n_out = rows // rb, out_rows // rb
    in_spec = pl.BlockSpec((None, rb, cols), lambda *g: (layer, jnp.minimum(step_of(*g), n_src - 1), 0))
    out_spec = pl.BlockSpec((rb, out_cols), lambda *g: (jnp.minimum(step_of(*g), n_out - 1), 0))
    return in_spec, out_spec, n_src, n_out


def _cast_row_block(rows, out_rows, max_blocks=None):
    sizes = [r for r in (16, 32, 64, 128, 256) if rows % r == 0 and out_rows % r == 0]
    if max_blocks is None:
        return sizes[-1]
    fits = [r for r in sizes if out_rows // r <= max_blocks]
    assert fits, "cast does not fit the host grid"
    return fits[0]


def _cast_kernel(src_ref, dst_ref, *, n_src):
    _cast_rows_block(src_ref, dst_ref, pl.program_id(0), n_src, pl.num_programs(0))


def cast_weight(w, layer, out_rows=None, out_cols=None):
    _, rows, cols = w.shape
    out_rows, out_cols = out_rows or rows, out_cols or cols
    rb = _cast_row_block(rows, out_rows)
    in_spec, out_spec, n_src, n_out = _cast_specs(w, layer, out_rows, out_cols, rb, lambda i: i)
    return pl.pallas_call(
        functools.partial(_cast_kernel, n_src=n_src),
        grid=(n_out,),
        in_specs=[in_spec],
        out_specs=out_spec,
        out_shape=jax.ShapeDtypeStruct((out_rows, out_cols), BF16),
        compiler_params=_params(1),
        name="cast_weight",
    )(w)


def _ffn_act_kernel(xg_ref, ssq_ref, wg_ref, wu_ref, cw_ref, *rest, sub, cast_blocks):
    n_cast = len(cast_blocks)
    cast_src = rest[:n_cast]
    o_ref = rest[n_cast]
    cast_dst = rest[n_cast + 1:2 * n_cast + 1]
    carry, r_s = rest[2 * n_cast + 1:]
    i, j = pl.program_id(0), pl.program_id(1)
    tm = xg_ref.shape[0]
    tf = o_ref.shape[1]
    taps = cw_ref.shape[0]

    @pl.when(j == 0)
    def _():
        r_s[...] = _row_scale(ssq_ref, xg_ref.shape[1])

    @pl.when(i == 0)
    def _():
        carry[j] = jnp.zeros(carry.shape[1:], F32)

    xg = xg_ref[...]
    r = r_s[...]
    for c0 in range(0, tf, sub):
        cs = slice(c0, c0 + sub)
        gate = _scale_rows(_dot(xg, wg_ref[:, cs]), r)
        up = _scale_rows(_dot(xg, wu_ref[:, cs]), r)
        ext = jnp.concatenate([carry[j, :, cs], gate], axis=0)
        carry[j, :, cs] = gate[tm - SUBLANES:, :]
        conv = cw_ref[taps - 1:taps, cs] * gate
        for s in range(1, taps):
            conv = conv + cw_ref[taps - 1 - s:taps - s, cs] * _shift_rows(ext, s, SUBLANES)
        o_ref[:, cs] = (_silu(conv) * up).astype(o_ref.dtype)

    step = i * pl.num_programs(1) + j
    for src, dst, (n_src, n_dst) in zip(cast_src, cast_dst, cast_blocks):
        _cast_rows_block(src, dst, step, n_src, n_dst)


def ffn_act(xg, ssq, w_gate, w_up, conv_w, casts=(), tm=ROW_TILE, tf=FFN_TILE):
    m, d = xg.shape
    f = w_gate.shape[1]
    taps = conv_w.shape[0]
    tm = min(tm, m)
    n_i, n_j = m // tm, f // tf
    parts = ssq.shape[0]
    assert taps - 1 <= SUBLANES
    in_specs = [pl.BlockSpec((tm, d), lambda i, j: (i, 0)),
                pl.BlockSpec((parts, tm, LANES), lambda i, j: (0, i, 0)),
                pl.BlockSpec((d, tf), lambda i, j: (0, j)),
                pl.BlockSpec((d, tf), lambda i, j: (0, j)),
                pl.BlockSpec((taps, tf), lambda i, j: (0, j))]
    out_specs = [pl.BlockSpec((tm, tf), lambda i, j: (i, j))]
    out_shape = [jax.ShapeDtypeStruct((m, f), BF16)]
    args = [xg, ssq, w_gate, w_up, conv_w]
    cast_in, cast_blocks = [], []
    for src, lyr, out_rows, out_cols in casts:
        rb = _cast_row_block(src.shape[1], out_rows, max_blocks=n_i * n_j)
        in_spec, out_spec, n_src, n_dst = _cast_specs(src, lyr, out_rows, out_cols, rb, lambda i, j: i * n_j + j)
        cast_in.append(in_spec)
        cast_blocks.append((n_src, n_dst))
        out_specs.append(out_spec)
        out_shape.append(jax.ShapeDtypeStruct((out_rows, out_cols), BF16))
        args.append(src)
    in_specs += cast_in
    res = pl.pallas_call(
        functools.partial(_ffn_act_kernel, sub=MXU_WIDTH, cast_blocks=tuple(cast_blocks)),
        grid=(n_i, n_j),
        in_specs=in_specs,
        out_specs=out_specs,
        out_shape=out_shape,
        scratch_shapes=[pltpu.VMEM((n_j, SUBLANES, tf), F32),
                        pltpu.VMEM((tm, LANES), F32)],
        compiler_params=_params(2),
        name="ffn_act",
    )(*args)
    return res[0], tuple(res[1:])


CONV_HALO = 32


def _mixer_a_kernel(av_ref, ag_ref, cw_ref, cb_ref, lg_ref, lb_ref, o_ref, ubuf, cbuf, *, taps, row_chunk):
    i = pl.program_id(0)
    tt, c = av_ref.shape

    @pl.when(i == 0)
    def _():
        ubuf[pl.ds(0, CONV_HALO), :] = jnp.zeros((CONV_HALO, c), F32)

    @pl.when(i > 0)
    def _():
        ubuf[pl.ds(0, CONV_HALO), :] = ubuf[pl.ds(tt, CONV_HALO), :]

    ubuf[pl.ds(CONV_HALO, tt), :] = av_ref[...] * jax.nn.sigmoid(ag_ref[...])

    def lane_chunk(l, _):
        l0 = pl.multiple_of(l * LANES, LANES)
        w = [cw_ref[j:j + 1, pl.ds(l0, LANES)] for j in range(taps)]
        bias = cb_ref[:, pl.ds(l0, LANES)]
        for r0 in range(0, tt, row_chunk):
            ext = ubuf[pl.ds(r0, CONV_HALO + row_chunk), pl.ds(l0, LANES)]
            acc = jnp.broadcast_to(bias, (row_chunk, LANES))
            for b in range(SUBLANES):
                rolled = ext if b == 0 else pltpu.roll(ext, b, axis=0)
                for a in range(CONV_HALO // SUBLANES):
                    s = SUBLANES * a + b
                    if s < taps:
                        top = CONV_HALO - SUBLANES * a
                        acc = acc + w[taps - 1 - s] * rolled[top:top + row_chunk]
            cbuf[pl.ds(r0, row_chunk), pl.ds(l0, LANES)] = acc
        return 0

    lax.fori_loop(0, c // LANES, lane_chunk, 0)

    def row_block(r, _):
        r0 = pl.multiple_of(r * row_chunk, row_chunk)
        y = cbuf[pl.ds(r0, row_chunk), :]
        yc = y - jnp.mean(y, axis=-1, keepdims=True)
        var = jnp.mean(yc * yc, axis=-1, keepdims=True)
        yn = yc * lax.rsqrt(var + NORM_EPS) * lg_ref[...] + lb_ref[...]
        o_ref[pl.ds(r0, row_chunk), :] = _silu(yn).astype(o_ref.dtype)
        return 0

    lax.fori_loop(0, tt // row_chunk, row_block, 0)


def mixer_a(proj, conv_w, conv_b, ln_g, ln_b, width, tt=256, row_chunk=64):
    m = proj.shape[0]
    taps = conv_w.shape[0]
    assert taps - 1 <= CONV_HALO
    tt = min(tt, m)
    row_chunk = min(row_chunk, tt)
    vec = lambda v: v.reshape(1, width).astype(F32)
    const = lambda i: (0, 0)
    return pl.pallas_call(
        functools.partial(_mixer_a_kernel, taps=taps, row_chunk=row_chunk),
        grid=(m // tt,),
        in_specs=[pl.BlockSpec((tt, width), lambda i: (i, 0)),
                  pl.BlockSpec((tt, width), lambda i: (i, 1)),
                  pl.BlockSpec((taps, width), const),
                  pl.BlockSpec((1, width), const),
                  pl.BlockSpec((1, width), const),
                  pl.BlockSpec((1, width), const)],
        out_specs=pl.BlockSpec((tt, width), lambda i: (i, 0)),
        out_shape=jax.ShapeDtypeStruct((m, 2 * width), BF16),
        scratch_shapes=[pltpu.VMEM((tt + CONV_HALO, width), F32),
                        pltpu.VMEM((tt, width), F32)],
        compiler_params=_params(1),
        name="mixer_a",
    )(proj, proj, conv_w.astype(F32), vec(conv_b), vec(ln_g), vec(ln_b))


def _deltanet_kernel(q_ref, k_ref, v_ref, z_ref, lg_ref, cwq_ref, cwk_ref, cwv_ref, alog_ref, dtb_ref, ng_ref,
                     mixed_hbm, o_ref, qbuf, kbuf, vbuf, state, lhs1, kn16, rhsuw, lhsb1, lhsb2, decay_s,
                     p_s, pw_s, n_s, u_s, oq_s, vnew16, *, taps, heads, head_groups):
    del mixed_hbm
    i = pl.program_id(0)
    c = q_ref.shape[0]
    halo = SUBLANES
    bufs = ((q_ref, qbuf), (k_ref, kbuf), (v_ref, vbuf))

    @pl.when(i == 0)
    def _():
        state[...] = jnp.zeros(state.shape, F32)
        for _, buf in bufs:
            buf[pl.ds(0, halo), :] = jnp.zeros((halo, buf.shape[1]), F32)

    @pl.when(i > 0)
    def _():
        for _, buf in bufs:
            buf[pl.ds(0, halo), :] = buf[pl.ds(c, halo), :]

    for ref, buf in bufs:
        buf[pl.ds(halo, c), :] = ref[...]

    row = lax.broadcasted_iota(jnp.int32, (c, c), 0)
    col = lax.broadcasted_iota(jnp.int32, (c, c), 1)
    lower = row >= col
    strict = row > col
    eye = (row == col).astype(F32)

    beta_all = jax.nn.sigmoid(lg_ref[:, 0:LANES])
    g_all = -jnp.exp(alog_ref[...]) * _softplus(lg_ref[:, LANES:2 * LANES] + dtb_ref[...])
    tri = lower.astype(BF16)
    g_hi = g_all.astype(BF16)
    r1 = g_all - g_hi.astype(F32)
    g_mid = r1.astype(BF16)
    g_lo = (r1 - g_mid.astype(F32)).astype(BF16)
    gc_all = _dot(tri, g_hi) + _dot(tri, g_mid) + _dot(tri, g_lo)
    gc_t = gc_all.T
    eg_all = jnp.exp(gc_all)
    glast_row = gc_all[c - 1:c, :]
    ek_all = jnp.exp(glast_row - gc_all)
    egl_row = jnp.exp(glast_row)

    def conv_silu(buf, cw_ref, hs):
        ext = buf[pl.ds(0, halo + c), hs]
        acc = cw_ref[taps - 1:taps, hs] * ext[halo:]
        for s in range(1, taps):
            acc = acc + cw_ref[taps - 1 - s:taps - s, hs] * _shift_rows(ext, s, halo)
        return _silu(acc)

    def prepare(h):
        hs = slice(h * HEAD_DIM, (h + 1) * HEAD_DIM)
        qh = conv_silu(qbuf, cwq_ref, hs)
        kh = conv_silu(kbuf, cwk_ref, hs)
        vh = conv_silu(vbuf, cwv_ref, hs)
        qn = qh * (lax.rsqrt(jnp.sum(qh * qh, axis=-1, keepdims=True) + NORM_EPS) * (HEAD_DIM ** -0.5))
        kn = kh * lax.rsqrt(jnp.sum(kh * kh, axis=-1, keepdims=True) + NORM_EPS)
        beta = beta_all[:, h:h + 1]
        gcol = gc_all[:, h:h + 1]
        grow = gc_t[h:h + 1, :]
        eg = eg_all[:, h:h + 1]
        kb = kn * beta
        decay_s[h] = jnp.where(lower, jnp.exp(jnp.where(lower, gcol - grow, 0.0)), 0.0)
        lhs1[h, 0:c, :] = kb.astype(BF16)
        lhs1[h, c:2 * c, :] = qn.astype(BF16)
        kn16[h] = kn.astype(BF16)
        rhsuw[h, :, 0:HEAD_DIM] = (vh * beta).astype(BF16)
        rhsuw[h, :, HEAD_DIM:2 * HEAD_DIM] = (kb * eg).astype(BF16)
        lhsb1[h, c:2 * c, :] = (qn * eg).astype(BF16)
        lhsb2[h, c:2 * c, :] = (kn * ek_all[:, h:h + 1]).T.astype(BF16)

    def matmul_stages(hl):
        for h in hl:
            a1 = _dot_nt(lhs1[h], kn16[h])
            dec = decay_s[h]
            neg_m = jnp.where(strict, -(a1[:c] * dec), 0.0)
            p_s[h] = eye + neg_m
            pw_s[h] = neg_m
            n_s[h] = neg_m
            lhsb2[h, 0:c, :] = (a1[c:] * dec).astype(BF16)

        for h in hl:
            n16 = pw_s[h].astype(BF16)
            pw_s[h] = _dot(n16, n16)
        span = 2
        while 2 * span < c:
            for h in hl:
                pw = pw_s[h]
                r = _dot(jnp.concatenate([p_s[h], pw], axis=0).astype(BF16), pw.astype(BF16))
                p_s[h] += r[:c]
                pw_s[h] = r[c:]
            span *= 2
        for h in hl:
            p = p_s[h]
            p_s[h] = p + _dot(p.astype(BF16), pw_s[h].astype(BF16))

        for h in hl:
            t0 = p_s[h]
            n_hi, n_lo = _split_bf16(n_s[h])
            t_hi, t_lo = _split_bf16(t0)
            pw_s[h] = (eye - t0) + (_dot(n_hi, t_hi) + _dot(n_hi, t_lo) + _dot(n_lo, t_hi))
        for h in hl:
            t0 = p_s[h]
            p_s[h] = t0 + _dot(t0.astype(BF16), pw_s[h].astype(BF16))

        for h in hl:
            uw = _dot(p_s[h].astype(BF16), rhsuw[h])
            u_s[h] = uw[:, :HEAD_DIM]
            lhsb1[h, 0:c, :] = uw[:, HEAD_DIM:].astype(BF16)

        for h in hl:
            b1 = _dot(lhsb1[h], state[h].astype(BF16))
            vnew16[h] = (u_s[h] - b1[:c]).astype(BF16)
            oq_s[h] = b1[c:]

        for h in hl:
            hs = slice(h * HEAD_DIM, (h + 1) * HEAD_DIM)
            b2 = _dot(lhsb2[h], vnew16[h])
            o = oq_s[h] + b2[:c]
            state[h] = state[h] * egl_row[:, h:h + 1] + b2[c:]
            on = o * lax.rsqrt(jnp.mean(o * o, axis=-1, keepdims=True) + NORM_EPS) * ng_ref[...]
            o_ref[:, hs] = (on * _silu(z_ref[:, hs])).astype(o_ref.dtype)

    per = heads // head_groups
    groups = [list(range(g * per, (g + 1) * per)) for g in range(head_groups)]
    for h in groups[0]:
        prepare(h)
    for g in range(head_groups):
        if g + 1 < head_groups:
            for h in groups[g + 1]:
                prepare(h)
        matmul_stages(groups[g])


def deltanet(proj, logits, conv_w, a_log, dt_bias, norm_g, mixed, width, col0):
    m = proj.shape[0]
    heads = width // HEAD_DIM
    taps = conv_w.shape[0]
    assert DN_BLOCK == HEAD_DIM and heads <= LANES and taps - 1 <= SUBLANES and m % DN_BLOCK == 0
    c = DN_BLOCK
    hd = HEAD_DIM
    head_groups = 2 if heads % 2 == 0 else 1
    pad = lambda v: jnp.pad(v.astype(F32), (0, LANES - heads)).reshape(1, LANES)
    const = lambda i: (0, 0)
    blk = lambda cb: pl.BlockSpec((c, width), lambda i, cb=cb: (i, cb))
    cw = conv_w.astype(F32)
    return pl.pallas_call(
        functools.partial(_deltanet_kernel, taps=taps, heads=heads, head_groups=head_groups),
        grid=(m // c,),
        in_specs=[blk(col0), blk(col0 + 1), blk(col0 + 2), blk(col0 + 3),
                  pl.BlockSpec((c, 2 * LANES), lambda i: (i, 0)),
                  pl.BlockSpec((taps, width), lambda i: (0, 0)),
                  pl.BlockSpec((taps, width), lambda i: (0, 1)),
                  pl.BlockSpec((taps, width), lambda i: (0, 2)),
                  pl.BlockSpec((1, LANES), const),
                  pl.BlockSpec((1, LANES), const),
                  pl.BlockSpec((1, hd), const),
                  pl.BlockSpec(memory_space=pl.ANY)],
        out_specs=pl.BlockSpec((c, width), lambda i: (i, 1)),
        out_shape=jax.ShapeDtypeStruct(mixed.shape, mixed.dtype),
        input_output_aliases={11: 0},
        scratch_shapes=[pltpu.VMEM((c + SUBLANES, width), F32)] * 3
                       + [pltpu.VMEM((heads, hd, hd), F32),
                        ] + [
                          pltpu.VMEM((heads, 2 * c, hd), BF16),
                          pltpu.VMEM((heads, c, hd), BF16),
                          pltpu.VMEM((heads, c, 2 * hd), BF16),
                          pltpu.VMEM((heads, 2 * c, hd), BF16),
                          pltpu.VMEM((heads, c + hd, c), BF16),
                          pltpu.VMEM((heads, c, c), F32),
                          pltpu.VMEM((heads, c, c), F32),
                          pltpu.VMEM((heads, c, c), F32),
                          pltpu.VMEM((heads, c, c), F32),
                          pltpu.VMEM((heads, c, hd), F32),
                          pltpu.VMEM((heads, c, hd), F32),
                          pltpu.VMEM((heads, c, hd), BF16)],
        compiler_params=_params(1),
        name="deltanet",
    )(proj, proj, proj, proj, logits, cw, cw, cw, pad(a_log), pad(dt_bias), norm_g.reshape(1, hd).astype(F32), mixed)


POOL_HALO = 16


def _odd_mix_kernel(gb_ref, gc_ref, val_ref, p_ref, cw_ref, pw_ref, ps_ref, o_ref, cvbuf, pbuf, *, taps):
    i = pl.program_id(0)
    tt, cw_width = gb_ref.shape
    pool_width = p_ref.shape[1]
    halo = SUBLANES

    @pl.when(i == 0)
    def _():
        cvbuf[pl.ds(0, halo), :] = jnp.zeros((halo, cw_width), F32)
        pbuf[pl.ds(0, POOL_HALO), :] = jnp.zeros((POOL_HALO, pool_width), F32)

    @pl.when(i > 0)
    def _():
        cvbuf[pl.ds(0, halo), :] = cvbuf[pl.ds(tt, halo), :]
        pbuf[pl.ds(0, POOL_HALO), :] = pbuf[pl.ds(tt, POOL_HALO), :]

    cvbuf[pl.ds(halo, tt), :] = gc_ref[...] * val_ref[...]
    pbuf[pl.ds(POOL_HALO, tt), :] = p_ref[...]

    conv = cw_ref[0:1, :] * cvbuf[pl.ds(halo - (taps - 1), tt), :]
    for j in range(1, taps):
        conv = conv + cw_ref[j:j + 1, :] * cvbuf[pl.ds(halo - (taps - 1) + j, tt), :]
    o_ref[:, 0:cw_width] = (gb_ref[...] * conv).astype(o_ref.dtype)

    groups = len(POOL_WINDOWS)
    gd = pool_width // groups
    t_count = (i * tt + 1 + lax.broadcasted_iota(jnp.int32, (tt, 1), 0)).astype(F32)
    for g, win in enumerate(POOL_WINDOWS):
        gs = slice(g * gd, (g + 1) * gd)
        self_rows = pbuf[pl.ds(POOL_HALO, tt), gs]
        acc = self_rows
        for back in range(1, win):
            acc = acc + pbuf[pl.ds(POOL_HALO - back, tt), gs]
        mixed = acc / jnp.minimum(t_count, float(win)) - self_rows
        d = _dot(mixed.astype(BF16), pw_ref[g]) * ps_ref[:, gs]
        o_ref[:, cw_width + g * gd:cw_width + (g + 1) * gd] = d.astype(o_ref.dtype)


def odd_mix(proj, conv_w, pool_w, pool_scale, width, tt=256):
    m = proj.shape[0]
    taps = conv_w.shape[0]
    groups, gd, _ = pool_w.shape
    assert groups == len(POOL_WINDOWS) and max(POOL_WINDOWS) - 1 <= POOL_HALO and taps - 1 <= SUBLANES
    tt = min(tt, m)
    const2 = lambda i: (0, 0)
    return pl.pallas_call(
        functools.partial(_odd_mix_kernel, taps=taps),
        grid=(m // tt,),
        in_specs=[pl.BlockSpec((tt, width), lambda i: (i, 0)),
                  pl.BlockSpec((tt, width), lambda i: (i, 1)),
                  pl.BlockSpec((tt, width), lambda i: (i, 2)),
                  pl.BlockSpec((tt, width), lambda i: (i, 3)),
                  pl.BlockSpec((taps, width), const2),
                  pl.BlockSpec((groups, gd, gd), lambda i: (0, 0, 0)),
                  pl.BlockSpec((1, width), const2)],
        out_specs=pl.BlockSpec((tt, 2 * width), lambda i: (i, 0)),
        out_shape=jax.ShapeDtypeStruct((m, 2 * width), BF16),
        scratch_shapes=[pltpu.VMEM((tt + SUBLANES, width), F32),
                        pltpu.VMEM((tt + POOL_HALO, width), F32)],
        compiler_params=_params(1),
        name="odd_mix",
    )(proj, proj, proj, proj, conv_w.astype(F32), pool_w.astype(BF16), pool_scale.reshape(1, width).astype(F32))


def kernel(x, mix_norm_g, ffn_norm_g, final_norm_g, ev_w_in, ev_w_out, a_conv_w, a_conv_b, a_ln_g, a_ln_b,
           dn_conv_w, dn_a_log, dn_dt_bias, dn_norm_g, od_w_in, od_w_out, sc_conv_w, pool_w, pool_scale,
           ffn_w_gate, ffn_conv_w, ffn_w_up, ffn_w_down):
    bsz, seq, d_model = x.shape
    assert bsz == 1, "sequence mixers carry state along the row axis; one sequence per call"
    depth = mix_norm_g.shape[0]
    half = d_model // 2
    heads = dn_a_log.shape[1]
    d_ff = ffn_w_gate.shape[2]
    ff_padded = d_ff + (-d_ff) % FFN_TILE
    xs = x.reshape(seq, d_model).astype(F32)

    conv_w = jnp.pad(ffn_conv_w.astype(F32), ((0, 0), (0, 0), (0, ff_padded - d_ff)))
    ev_in = ev_w_in.astype(BF16)
    mix_w = (None, cast_weight(ev_w_out, 0))
    ffn_w = (cast_weight(ffn_w_gate, 0, out_cols=ff_padded), cast_weight(ffn_w_up, 0, out_cols=ff_padded),
             cast_weight(ffn_w_down, 0, out_rows=ff_padded))

    xg, ssq = norm_inputs(xs, mix_norm_g[0])
    for layer in range(depth):
        j = layer // 2
        w_in, w_out = mix_w
        if layer % 2 == 0:
            w_log = ev_in[j, :, 6 * half:]
            lane_pad = ((0, 0), (0, LANES - heads))
            w_log = jnp.concatenate([jnp.pad(w_log[:, :heads], lane_pad), jnp.pad(w_log[:, heads:], lane_pad)], axis=1)
            proj = proj_matmul(xg, ssq, ev_in, 6 * half, tn=1024, layer=j)
            logits = proj_matmul(xg, ssq, w_log, 2 * LANES, tn=2 * LANES)
            mixed = mixer_a(proj, a_conv_w[j], a_conv_b[j], a_ln_g[j], a_ln_b[j], half)
            mixed = deltanet(proj, logits, dn_conv_w[j], dn_a_log[j], dn_dt_bias[j], dn_norm_g[j], mixed, half, col0=2)
        else:
            proj = proj_matmul(xg, ssq, w_in, 4 * half, tn=1024)
            mixed = odd_mix(proj, sc_conv_w[j], pool_w[j], pool_scale[j], half)
        xs, xg, ssq = out_proj(mixed, w_out, xs, ffn_norm_g[layer])
        if layer + 1 == depth:
            act, _ = ffn_act(xg, ssq, ffn_w[0], ffn_w[1], conv_w[layer])
            xs, _, _ = ffn_down(act, ffn_w[2], xs, None)
        else:
            nxt = layer + 1
            casts = [(ffn_w_gate, nxt, d_model, ff_padded), (ffn_w_up, nxt, d_model, ff_padded),
                     (ffn_w_down, nxt, ff_padded, d_model)]
            if nxt % 2 == 0:
                casts += [(ev_w_out, nxt // 2) + ev_w_out.shape[1:]]
            else:
                casts += [(od_w_out, nxt // 2) + od_w_out.shape[1:], (od_w_in, nxt // 2) + od_w_in.shape[1:]]
            act, cast = ffn_act(xg, ssq, ffn_w[0], ffn_w[1], conv_w[layer], casts)
            xs, xg, ssq = ffn_down(act, ffn_w[2], xs, mix_norm_g[nxt])
            ffn_w = cast[:3]
            mix_w = (cast[4] if nxt % 2 else None, cast[3])
    return rmsnorm(xs, final_norm_g, x.dtype).reshape(bsz, seq, d_model)
```

```python
import functools

import jax
import jax.numpy as jnp
from jax import lax
from jax.experimental import pallas as pl
from jax.experimental.pallas import tpu as pltpu

F32 = jnp.float32
BF16 = jnp.bfloat16
NORM_EPS = 1e-6
LANES = 128
SUBLANES = 8
MXU_WIDTH = 256
HEAD_DIM = 128
DN_BLOCK = 128
POOL_WINDOWS = (2, 4, 8, 16)
VMEM_LIMIT_BYTES = 60000 * 1024
FFN_TILE = 512
ROW_TILE = 1024


def _params(n_axes):
    return pltpu.CompilerParams(dimension_semantics=("arbitrary",) * n_axes,
                                vmem_limit_bytes=VMEM_LIMIT_BYTES)


def _silu(x):
    return x * jax.nn.sigmoid(x)


def _softplus(x):
    return jnp.maximum(x, 0.0) + jnp.log1p(jnp.exp(-jnp.abs(x)))


def _dot(a, b):
    return jnp.dot(a, b, preferred_element_type=F32)


def _dot_nt(a, b):
    return lax.dot_general(a, b, (((1,), (1,)), ((), ())), preferred_element_type=F32)


def _split_bf16(x):
    hi = x.astype(BF16)
    return hi, (x - hi.astype(F32)).astype(BF16)


def _lane_chunks(v):
    return [v[:, c0:c0 + LANES] for c0 in range(0, v.shape[1], LANES)]


def _scale_rows(v, r):
    return jnp.concatenate([ch * r for ch in _lane_chunks(v)], axis=1)


def _row_scale(ssq_ref, d_model):
    tot = jnp.sum(ssq_ref[...], axis=0)
    ms = jnp.sum(tot, axis=-1, keepdims=True) * (1.0 / d_model)
    return jnp.broadcast_to(lax.rsqrt(ms + NORM_EPS), tot.shape)


def _emit_norm_inputs(x_new, g_ref, xg_ref):
    xg_ref[...] = (x_new * g_ref[...]).astype(xg_ref.dtype)
    return functools.reduce(lambda a, b: a + b, _lane_chunks(x_new * x_new))


def _shift_rows(ext, shift, halo):
    return pltpu.roll(ext, shift, axis=0)[halo:]


def _rmsnorm_kernel(x_ref, g_ref, o_ref):
    x = x_ref[...]
    ms = jnp.mean(x * x, axis=-1, keepdims=True)
    o_ref[...] = (x * lax.rsqrt(ms + NORM_EPS) * g_ref[...]).astype(o_ref.dtype)


def rmsnorm(x, g, out_dtype, tm=256):
    m, d = x.shape
    return pl.pallas_call(
        _rmsnorm_kernel,
        grid=(m // tm,),
        in_specs=[pl.BlockSpec((tm, d), lambda i: (i, 0)),
                  pl.BlockSpec((1, d), lambda i: (0, 0))],
        out_specs=pl.BlockSpec((tm, d), lambda i: (i, 0)),
        out_shape=jax.ShapeDtypeStruct((m, d), out_dtype),
        compiler_params=_params(1),
        name="rmsnorm",
    )(x, g.reshape(1, d).astype(F32))


def _norm_inputs_kernel(x_ref, g_ref, xg_ref, ssq_ref):
    ssq_ref[...] = _emit_norm_inputs(x_ref[...], g_ref, xg_ref)


def norm_inputs(x, g, tm=256):
    m, d = x.shape
    return pl.pallas_call(
        _norm_inputs_kernel,
        grid=(m // tm,),
        in_specs=[pl.BlockSpec((tm, d), lambda i: (i, 0)),
                  pl.BlockSpec((1, d), lambda i: (0, 0))],
        out_specs=[pl.BlockSpec((tm, d), lambda i: (i, 0)),
                   pl.BlockSpec((None, tm, LANES), lambda i: (0, i, 0))],
        out_shape=[jax.ShapeDtypeStruct((m, d), BF16),
                   jax.ShapeDtypeStruct((1, m, LANES), F32)],
        compiler_params=_params(1),
        name="norm_inputs",
    )(x, g.reshape(1, d).astype(F32))


def _proj_kernel(xg_ref, ssq_ref, w_ref, o_ref, r_s):
    @pl.when(pl.program_id(1) == 0)
    def _():
        r_s[...] = _row_scale(ssq_ref, xg_ref.shape[1])

    for c0 in range(0, o_ref.shape[1], MXU_WIDTH):
        cs = slice(c0, c0 + MXU_WIDTH)
        o_ref[:, cs] = _scale_rows(_dot(xg_ref[...], w_ref[:, cs]), r_s[...]).astype(o_ref.dtype)


def proj_matmul(xg, ssq, w, n_cols, tn, layer=None, tm=ROW_TILE):
    m, k = xg.shape
    tm, tn = min(tm, m), min(tn, n_cols)
    parts = ssq.shape[0]
    if layer is None:
        w_spec = pl.BlockSpec((k, tn), lambda i, j: (0, j))
    else:
        w_spec = pl.BlockSpec((None, k, tn), lambda i, j: (layer, 0, j))
    return pl.pallas_call(
        _proj_kernel,
        grid=(m // tm, n_cols // tn),
        in_specs=[pl.BlockSpec((tm, k), lambda i, j: (i, 0)),
                  pl.BlockSpec((parts, tm, LANES), lambda i, j: (0, i, 0)),
                  w_spec],
        out_specs=pl.BlockSpec((tm, tn), lambda i, j: (i, j)),
        out_shape=jax.ShapeDtypeStruct((m, n_cols), F32),
        scratch_shapes=[pltpu.VMEM((tm, LANES), F32)],
        compiler_params=_params(2),
        name="proj_matmul",
    )(xg, ssq, w)


def _out_proj_kernel(a_ref, w_ref, x_ref, g_ref, o_ref, xg_ref, ssq_ref):
    j = pl.program_id(1)
    part = None
    for c0 in range(0, o_ref.shape[1], MXU_WIDTH):
        cs = slice(c0, c0 + MXU_WIDTH)
        x_new = x_ref[:, cs] + _dot(a_ref[...], w_ref[:, cs])
        o_ref[:, cs] = x_new
        sq = _emit_norm_inputs(x_new, g_ref.at[:, cs], xg_ref.at[:, cs])
        part = sq if part is None else part + sq

    @pl.when(j == 0)
    def _():
        ssq_ref[...] = part

    @pl.when(j > 0)
    def _():
        ssq_ref[...] += part


def out_proj(a, w, x, g_next, tm=ROW_TILE, tn=1024):
    m, k = a.shape
    n = w.shape[1]
    tm, tn = min(tm, m), min(tn, n)
    return pl.pallas_call(
        _out_proj_kernel,
        grid=(m // tm, n // tn),
        in_specs=[pl.BlockSpec((tm, k), lambda i, j: (i, 0)),
                  pl.BlockSpec((k, tn), lambda i, j: (0, j)),
                  pl.BlockSpec((tm, tn), lambda i, j: (i, j)),
                  pl.BlockSpec((1, tn), lambda i, j: (0, j))],
        out_specs=[pl.BlockSpec((tm, tn), lambda i, j: (i, j)),
                   pl.BlockSpec((tm, tn), lambda i, j: (i, j)),
                   pl.BlockSpec((None, tm, LANES), lambda i, j: (0, i, 0))],
        out_shape=[jax.ShapeDtypeStruct((m, n), F32),
                   jax.ShapeDtypeStruct((m, n), BF16),
                   jax.ShapeDtypeStruct((1, m, LANES), F32)],
        compiler_params=_params(2),
        name="out_proj",
    )(a, w, x, g_next.reshape(1, n).astype(F32))


def _ffn_down_kernel(a_ref, w_ref, x_ref, g_ref, o_ref, *norm_refs):
    kk = pl.program_id(2)

    @pl.when(kk == 0)
    def _():
        o_ref[...] = x_ref[...]

    o_ref[...] += _dot(a_ref[...], w_ref[...])

    if norm_refs:
        xg_ref, ssq_ref = norm_refs

        @pl.when(kk == pl.num_programs(2) - 1)
        def _():
            ssq_ref[...] = _emit_norm_inputs(o_ref[...], g_ref, xg_ref)


def ffn_down(a, w, x, g_next, tm=ROW_TILE, tn=2048, tk=1024):
    m, k = a.shape
    n = w.shape[1]
    tm, tn, tk = min(tm, m), min(tn, n), min(tk, k)
    emit = g_next is not None
    g = (g_next if emit else jnp.ones((n,), F32)).reshape(1, n).astype(F32)
    blk = pl.BlockSpec((tm, tn), lambda j, i, kk: (i, j))
    out_specs = [blk]
    out_shape = [jax.ShapeDtypeStruct((m, n), F32)]
    if emit:
        out_specs += [blk, pl.BlockSpec((None, tm, LANES), lambda j, i, kk: (j, i, 0))]
        out_shape += [jax.ShapeDtypeStruct((m, n), BF16), jax.ShapeDtypeStruct((n // tn, m, LANES), F32)]
    res = pl.pallas_call(
        _ffn_down_kernel,
        grid=(n // tn, m // tm, k // tk),
        in_specs=[pl.BlockSpec((tm, tk), lambda j, i, kk: (i, kk)),
                  pl.BlockSpec((tk, tn), lambda j, i, kk: (kk, j)),
                  blk,
                  pl.BlockSpec((1, tn), lambda j, i, kk: (0, j))],
        out_specs=out_specs,
        out_shape=out_shape,
        compiler_params=_params(3),
        name="ffn_down",
    )(a, w, x, g)
    return res if emit else (res[0], None, None)


BF16_SUBLANES = 16


def _cast_rows_block(src_ref, dst_ref, step, n_src_blocks, n_dst_blocks):
    t = jnp.minimum(step, n_dst_blocks - 1)
    v = src_ref[...].astype(dst_ref.dtype)
    v = jnp.where(t < n_src_blocks, v, jnp.zeros_like(v))
    cols = v.shape[1]
    dst_ref[:, 0:cols] = v
    if cols < dst_ref.shape[1]:
        dst_ref[:, cols:] = jnp.zeros((v.shape[0], dst_ref.shape[1] - cols), dst_ref.dtype)


def _cast_specs(src, layer, out_rows, out_cols, rb, step_of):
    _, rows, cols = src.shape
    assert rows % rb == 0 and out_rows % rb == 0 and rb % BF16_SUBLANES == 0 and cols <= out_cols
    n_src, n_out = rows // rb, out_rows // rb
    in_spec = pl.BlockSpec((None, rb, cols), lambda *g: (layer, jnp.minimum(step_of(*g), n_src - 1), 0))
    out_spec = pl.BlockSpec((rb, out_cols), lambda *g: (jnp.minimum(step_of(*g), n_out - 1), 0))
    return in_spec, out_spec, n_src, n_out


def _cast_row_block(rows, out_rows, max_blocks=None):
    sizes = [r for r in (16, 32, 64, 128, 256) if rows % r == 0 and out_rows % r == 0]
    if max_blocks is None:
        return sizes[-1]
    fits = [r for r in sizes if out_rows // r <= max_blocks]
    assert fits, "cast does not fit the host grid"
    return fits[0]


def _cast_kernel(src_ref, dst_ref, *, n_src):
    _cast_rows_block(src_ref, dst_ref, pl.program_id(0), n_src, pl.num_programs(0))


def cast_weight(w, layer, out_rows=None, out_cols=None):
    _, rows, cols = w.shape
    out_rows, out_cols = out_rows or rows, out_cols or cols
    rb = _cast_row_block(rows, out_rows)
    in_spec, out_spec, n_src, n_out = _cast_specs(w, layer, out_rows, out_cols, rb, lambda i: i)
    return pl.pallas_call(
        functools.partial(_cast_kernel, n_src=n_src),
        grid=(n_out,),
        in_specs=[in_spec],
        out_specs=out_spec,
        out_shape=jax.ShapeDtypeStruct((out_rows, out_cols), BF16),
        compiler_params=_params(1),
        name="cast_weight",
    )(w)


def _ffn_act_kernel(xg_ref, ssq_ref, wg_ref, wu_ref, cw_ref, *rest, sub, cast_blocks):
    n_cast = len(cast_blocks)
    cast_src = rest[:n_cast]
    o_ref = rest[n_cast]
    cast_dst = rest[n_cast + 1:2 * n_cast + 1]
    carry, r_s = rest[2 * n_cast + 1:]
    i, j = pl.program_id(0), pl.program_id(1)
    tm = xg_ref.shape[0]
    tf = o_ref.shape[1]
    taps = cw_ref.shape[0]

    @pl.when(j == 0)
    def _():
        r_s[...] = _row_scale(ssq_ref, xg_ref.shape[1])

    @pl.when(i == 0)
    def _():
        carry[j] = jnp.zeros(carry.shape[1:], F32)

    xg = xg_ref[...]
    r = r_s[...]
    for c0 in range(0, tf, sub):
        cs = slice(c0, c0 + sub)
        gate = _scale_rows(_dot(xg, wg_ref[:, cs]), r)
        up = _scale_rows(_dot(xg, wu_ref[:, cs]), r)
        ext = jnp.concatenate([carry[j, :, cs], gate], axis=0)
        carry[j, :, cs] = gate[tm - SUBLANES:, :]
        conv = cw_ref[taps - 1:taps, cs] * gate
        for s in range(1, taps):
            conv = conv + cw_ref[taps - 1 - s:taps - s, cs] * _shift_rows(ext, s, SUBLANES)
        o_ref[:, cs] = (_silu(conv) * up).astype(o_ref.dtype)

    step = i * pl.num_programs(1) + j
    for src, dst, (n_src, n_dst) in zip(cast_src, cast_dst, cast_blocks):
        _cast_rows_block(src, dst, step, n_src, n_dst)


def ffn_act(xg, ssq, w_gate, w_up, conv_w, casts=(), tm=ROW_TILE, tf=FFN_TILE):
    m, d = xg.shape
    f = w_gate.shape[1]
    taps = conv_w.shape[0]
    tm = min(tm, m)
    n_i, n_j = m // tm, f // tf
    parts = ssq.shape[0]
    assert taps - 1 <= SUBLANES
    in_specs = [pl.BlockSpec((tm, d), lambda i, j: (i, 0)),
                pl.BlockSpec((parts, tm, LANES), lambda i, j: (0, i, 0)),
                pl.BlockSpec((d, tf), lambda i, j: (0, j)),
                pl.BlockSpec((d, tf), lambda i, j: (0, j)),
                pl.BlockSpec((taps, tf), lambda i, j: (0, j))]
    out_specs = [pl.BlockSpec((tm, tf), lambda i, j: (i, j))]
    out_shape = [jax.ShapeDtypeStruct((m, f), BF16)]
    args = [xg, ssq, w_gate, w_up, conv_w]
    cast_in, cast_blocks = [], []
    for src, lyr, out_rows, out_cols in casts:
        rb = _cast_row_block(src.shape[1], out_rows, max_blocks=n_i * n_j)
        in_spec, out_spec, n_src, n_dst = _cast_specs(src, lyr, out_rows, out_cols, rb, lambda i, j: i * n_j + j)
        cast_in.append(in_spec)
        cast_blocks.append((n_src, n_dst))
        out_specs.append(out_spec)
        out_shape.append(jax.ShapeDtypeStruct((out_rows, out_cols), BF16))
        args.append(src)
    in_specs += cast_in
    res = pl.pallas_call(
        functools.partial(_ffn_act_kernel, sub=MXU_WIDTH, cast_blocks=tuple(cast_blocks)),
        grid=(n_i, n_j),
        in_specs=in_specs,
        out_specs=out_specs,
        out_shape=out_shape,
        scratch_shapes=[pltpu.VMEM((n_j, SUBLANES, tf), F32),
                        pltpu.VMEM((tm, LANES), F32)],
        compiler_params=_params(2),
        name="ffn_act",
    )(*args)
    return res[0], tuple(res[1:])


CONV_HALO = 32


def _mixer_a_kernel(av_ref, ag_ref, cw_ref, cb_ref, lg_ref, lb_ref, o_ref, ubuf, cbuf, *, taps, row_chunk):
    i = pl.program_id(0)
    tt, c = av_ref.shape

    @pl.when(i == 0)
    def _():
        ubuf[pl.ds(0, CONV_HALO), :] = jnp.zeros((CONV_HALO, c), F32)

    @pl.when(i > 0)
    def _():
        ubuf[pl.ds(0, CONV_HALO), :] = ubuf[pl.ds(tt, CONV_HALO), :]

    ubuf[pl.ds(CONV_HALO, tt), :] = av_ref[...] * jax.nn.sigmoid(ag_ref[...])

    def lane_chunk(l, _):
        l0 = pl.multiple_of(l * LANES, LANES)
        w = [cw_ref[j:j + 1, pl.ds(l0, LANES)] for j in range(taps)]
        bias = cb_ref[:, pl.ds(l0, LANES)]
        for r0 in range(0, tt, row_chunk):
            ext = ubuf[pl.ds(r0, CONV_HALO + row_chunk), pl.ds(l0, LANES)]
            acc = jnp.broadcast_to(bias, (row_chunk, LANES))
            for b in range(SUBLANES):
                rolled = ext if b == 0 else pltpu.roll(ext, b, axis=0)
                for a in range(CONV_HALO // SUBLANES):
                    s = SUBLANES * a + b
                    if s < taps:
                        top = CONV_HALO - SUBLANES * a
                        acc = acc + w[taps - 1 - s] * rolled[top:top + row_chunk]
            cbuf[pl.ds(r0, row_chunk), pl.ds(l0, LANES)] = acc
        return 0

    lax.fori_loop(0, c // LANES, lane_chunk, 0)

    def row_block(r, _):
        r0 = pl.multiple_of(r * row_chunk, row_chunk)
        y = cbuf[pl.ds(r0, row_chunk), :]
        yc = y - jnp.mean(y, axis=-1, keepdims=True)
        var = jnp.mean(yc * yc, axis=-1, keepdims=True)
        yn = yc * lax.rsqrt(var + NORM_EPS) * lg_ref[...] + lb_ref[...]
        o_ref[pl.ds(r0, row_chunk), :] = _silu(yn).astype(o_ref.dtype)
        return 0

    lax.fori_loop(0, tt // row_chunk, row_block, 0)


def mixer_a(proj, conv_w, conv_b, ln_g, ln_b, width, tt=256, row_chunk=128):
    m = proj.shape[0]
    taps = conv_w.shape[0]
    assert taps - 1 <= CONV_HALO
    tt = min(tt, m)
    row_chunk = min(row_chunk, tt)
    vec = lambda v: v.reshape(1, width).astype(F32)
    const = lambda i: (0, 0)
    return pl.pallas_call(
        functools.partial(_mixer_a_kernel, taps=taps, row_chunk=row_chunk),
        grid=(m // tt,),
        in_specs=[pl.BlockSpec((tt, width), lambda i: (i, 0)),
                  pl.BlockSpec((tt, width), lambda i: (i, 1)),
                  pl.BlockSpec((taps, width), const),
                  pl.BlockSpec((1, width), const),
                  pl.BlockSpec((1, width), const),
                  pl.BlockSpec((1, width), const)],
        out_specs=pl.BlockSpec((tt, width), lambda i: (i, 0)),
        out_shape=jax.ShapeDtypeStruct((m, 2 * width), BF16),
        scratch_shapes=[pltpu.VMEM((tt + CONV_HALO, width), F32),
                        pltpu.VMEM((tt, width), F32)],
        compiler_params=_params(1),
        name="mixer_a",
    )(proj, proj, conv_w.astype(F32), vec(conv_b), vec(ln_g), vec(ln_b))


def _deltanet_kernel(q_ref, k_ref, v_ref, z_ref, lg_ref, cwq_ref, cwk_ref, cwv_ref, alog_ref, dtb_ref, ng_ref,
                     mixed_hbm, o_ref, qbuf, kbuf, vbuf, state, lhs1, kn16, rhsuw, lhsb1, lhsb2, decay_s,
                     p_s, pw_s, n_s, u_s, oq_s, vnew16, *, taps, heads, head_groups):
    del mixed_hbm
    i = pl.program_id(0)
    c = q_ref.shape[0]
    halo = SUBLANES
    bufs = ((q_ref, qbuf), (k_ref, kbuf), (v_ref, vbuf))

    @pl.when(i == 0)
    def _():
        state[...] = jnp.zeros(state.shape, F32)
        for _, buf in bufs:
            buf[pl.ds(0, halo), :] = jnp.zeros((halo, buf.shape[1]), F32)

    @pl.when(i > 0)
    def _():
        for _, buf in bufs:
            buf[pl.ds(0, halo), :] = buf[pl.ds(c, halo), :]

    for ref, buf in bufs:
        buf[pl.ds(halo, c), :] = ref[...]

    row = lax.broadcasted_iota(jnp.int32, (c, c), 0)
    col = lax.broadcasted_iota(jnp.int32, (c, c), 1)
    lower = row >= col
    strict = row > col
    eye = (row == col).astype(F32)

    beta_all = jax.nn.sigmoid(lg_ref[:, 0:LANES])
    g_all = -jnp.exp(alog_ref[...]) * _softplus(lg_ref[:, LANES:2 * LANES] + dtb_ref[...])
    tri = lower.astype(BF16)
    g_hi = g_all.astype(BF16)
    r1 = g_all - g_hi.astype(F32)
    g_mid = r1.astype(BF16)
    g_lo = (r1 - g_mid.astype(F32)).astype(BF16)
    gc_all = _dot(tri, g_hi) + _dot(tri, g_mid) + _dot(tri, g_lo)
    gc_t = gc_all.T
    eg_all = jnp.exp(gc_all)
    glast_row = gc_all[c - 1:c, :]
    ek_all = jnp.exp(glast_row - gc_all)
    egl_row = jnp.exp(glast_row)

    def conv_silu(buf, cw_ref, hs):
        ext = buf[pl.ds(0, halo + c), hs]
        acc = cw_ref[taps - 1:taps, hs] * ext[halo:]
        for s in range(1, taps):
            acc = acc + cw_ref[taps - 1 - s:taps - s, hs] * _shift_rows(ext, s, halo)
        return _silu(acc)

    def prepare(h):
        hs = slice(h * HEAD_DIM, (h + 1) * HEAD_DIM)
        qh = conv_silu(qbuf, cwq_ref, hs)
        kh = conv_silu(kbuf, cwk_ref, hs)
        vh = conv_silu(vbuf, cwv_ref, hs)
        qn = qh * (lax.rsqrt(jnp.sum(qh * qh, axis=-1, keepdims=True) + NORM_EPS) * (HEAD_DIM ** -0.5))
        kn = kh * lax.rsqrt(jnp.sum(kh * kh, axis=-1, keepdims=True) + NORM_EPS)
        beta = beta_all[:, h:h + 1]
        gcol = gc_all[:, h:h + 1]
        grow = gc_t[h:h + 1, :]
        eg = eg_all[:, h:h + 1]
        kb = kn * beta
        decay_s[h] = jnp.where(lower, jnp.exp(jnp.where(lower, gcol - grow, 0.0)), 0.0)
        lhs1[h, 0:c, :] = kb.astype(BF16)
        lhs1[h, c:2 * c, :] = qn.astype(BF16)
        kn16[h] = kn.astype(BF16)
        rhsuw[h, :, 0:HEAD_DIM] = (vh * beta).astype(BF16)
        rhsuw[h, :, HEAD_DIM:2 * HEAD_DIM] = (kb * eg).astype(BF16)
        lhsb1[h, c:2 * c, :] = (qn * eg).astype(BF16)
        lhsb2[h, c:2 * c, :] = (kn * ek_all[:, h:h + 1]).T.astype(BF16)

    def matmul_stages(hl):
        for h in hl:
            a1 = _dot_nt(lhs1[h], kn16[h])
            dec = decay_s[h]
            neg_m = jnp.where(strict, -(a1[:c] * dec), 0.0)
            p_s[h] = eye + neg_m
            pw_s[h] = neg_m
            n_s[h] = neg_m
            lhsb2[h, 0:c, :] = (a1[c:] * dec).astype(BF16)

        for h in hl:
            n16 = pw_s[h].astype(BF16)
            pw_s[h] = _dot(n16, n16)
        span = 2
        while 4 * span < c:
            for h in hl:
                pw = pw_s[h]
                r = _dot(jnp.concatenate([p_s[h], pw], axis=0).astype(BF16), pw.astype(BF16))
                p_s[h] += r[:c]
                pw_s[h] = r[c:]
            span *= 2
        for h in hl:
            p = p_s[h]
            p_s[h] = p + _dot(p.astype(BF16), pw_s[h].astype(BF16))

        for h in hl:
            t0 = p_s[h]
            n_hi, n_lo = _split_bf16(n_s[h])
            t_hi, t_lo = _split_bf16(t0)
            pw_s[h] = (eye - t0) + (_dot(n_hi, t_hi) + _dot(n_hi, t_lo) + _dot(n_lo, t_hi))
        for h in hl:
            t0 = p_s[h]
            p_s[h] = t0 + _dot(t0.astype(BF16), pw_s[h].astype(BF16))

        for h in hl:
            uw = _dot(p_s[h].astype(BF16), rhsuw[h])
            u_s[h] = uw[:, :HEAD_DIM]
            lhsb1[h, 0:c, :] = uw[:, HEAD_DIM:].astype(BF16)

        for h in hl:
            b1 = _dot(lhsb1[h], state[h].astype(BF16))
            vnew16[h] = (u_s[h] - b1[:c]).astype(BF16)
            oq_s[h] = b1[c:]

        for h in hl:
            hs = slice(h * HEAD_DIM, (h + 1) * HEAD_DIM)
            b2 = _dot(lhsb2[h], vnew16[h])
            o = oq_s[h] + b2[:c]
            state[h] = state[h] * egl_row[:, h:h + 1] + b2[c:]
            on = o * lax.rsqrt(jnp.mean(o * o, axis=-1, keepdims=True) + NORM_EPS) * ng_ref[...]
            o_ref[:, hs] = (on * _silu(z_ref[:, hs])).astype(o_ref.dtype)

    per = heads // head_groups
    groups = [list(range(g * per, (g + 1) * per)) for g in range(head_groups)]
    for h in groups[0]:
        prepare(h)
    for g in range(head_groups):
        if g + 1 < head_groups:
            for h in groups[g + 1]:
                prepare(h)
        matmul_stages(groups[g])


def deltanet(proj, logits, conv_w, a_log, dt_bias, norm_g, mixed, width, col0):
    m = proj.shape[0]
    heads = width // HEAD_DIM
    taps = conv_w.shape[0]
    assert DN_BLOCK == HEAD_DIM and heads <= LANES and taps - 1 <= SUBLANES and m % DN_BLOCK == 0
    c = DN_BLOCK
    hd = HEAD_DIM
    head_groups = 2 if heads % 2 == 0 else 1
    pad = lambda v: jnp.pad(v.astype(F32), (0, LANES - heads)).reshape(1, LANES)
    const = lambda i: (0, 0)
    blk = lambda cb: pl.BlockSpec((c, width), lambda i, cb=cb: (i, cb))
    cw = conv_w.astype(F32)
    return pl.pallas_call(
        functools.partial(_deltanet_kernel, taps=taps, heads=heads, head_groups=head_groups),
        grid=(m // c,),
        in_specs=[blk(col0), blk(col0 + 1), blk(col0 + 2), blk(col0 + 3),
                  pl.BlockSpec((c, 2 * LANES), lambda i: (i, 0)),
                  pl.BlockSpec((taps, width), lambda i: (0, 0)),
                  pl.BlockSpec((taps, width), lambda i: (0, 1)),
                  pl.BlockSpec((taps, width), lambda i: (0, 2)),
                  pl.BlockSpec((1, LANES), const),
                  pl.BlockSpec((1, LANES), const),
                  pl.BlockSpec((1, hd), const),
                  pl.BlockSpec(memory_space=pl.ANY)],
        out_specs=pl.BlockSpec((c, width), lambda i: (i, 1)),
        out_shape=jax.ShapeDtypeStruct(mixed.shape, mixed.dtype),
        input_output_aliases={11: 0},
        scratch_shapes=[pltpu.VMEM((c + SUBLANES, width), F32)] * 3
                       + [pltpu.VMEM((heads, hd, hd), F32),
                        ] + [
                          pltpu.VMEM((heads, 2 * c, hd), BF16),
                          pltpu.VMEM((heads, c, hd), BF16),
                          pltpu.VMEM((heads, c, 2 * hd), BF16),
                          pltpu.VMEM((heads, 2 * c, hd), BF16),
                          pltpu.VMEM((heads, c + hd, c), BF16),
                          pltpu.VMEM((heads, c, c), F32),
                          pltpu.VMEM((heads, c, c), F32),
                          pltpu.VMEM((heads, c, c), F32),
                          pltpu.VMEM((heads, c, c), F32),
                          pltpu.VMEM((heads, c, hd), F32),
                          pltpu.VMEM((heads, c, hd), F32),
                          pltpu.VMEM((heads, c, hd), BF16)],
        compiler_params=_params(1),
        name="deltanet",
    )(proj, proj, proj, proj, logits, cw, cw, cw, pad(a_log), pad(dt_bias), norm_g.reshape(1, hd).astype(F32), mixed)


POOL_HALO = 16


def _odd_mix_kernel(gb_ref, gc_ref, val_ref, p_ref, cw_ref, pw_ref, ps_ref, o_ref, cvbuf, pbuf, *, taps):
    i = pl.program_id(0)
    tt, cw_width = gb_ref.shape
    pool_width = p_ref.shape[1]
    halo = SUBLANES

    @pl.when(i == 0)
    def _():
        cvbuf[pl.ds(0, halo), :] = jnp.zeros((halo, cw_width), F32)
        pbuf[pl.ds(0, POOL_HALO), :] = jnp.zeros((POOL_HALO, pool_width), F32)

    @pl.when(i > 0)
    def _():
        cvbuf[pl.ds(0, halo), :] = cvbuf[pl.ds(tt, halo), :]
        pbuf[pl.ds(0, POOL_HALO), :] = pbuf[pl.ds(tt, POOL_HALO), :]

    cvbuf[pl.ds(halo, tt), :] = gc_ref[...] * val_ref[...]
    pbuf[pl.ds(POOL_HALO, tt), :] = p_ref[...]

    conv = cw_ref[0:1, :] * cvbuf[pl.ds(halo - (taps - 1), tt), :]
    for j in range(1, taps):
        conv = conv + cw_ref[j:j + 1, :] * cvbuf[pl.ds(halo - (taps - 1) + j, tt), :]
    o_ref[:, 0:cw_width] = (gb_ref[...] * conv).astype(o_ref.dtype)

    groups = len(POOL_WINDOWS)
    gd = pool_width // groups
    t_count = (i * tt + 1 + lax.broadcasted_iota(jnp.int32, (tt, 1), 0)).astype(F32)
    for g, win in enumerate(POOL_WINDOWS):
        gs = slice(g * gd, (g + 1) * gd)
        self_rows = pbuf[pl.ds(POOL_HALO, tt), gs]
        acc = self_rows
        for back in range(1, win):
            acc = acc + pbuf[pl.ds(POOL_HALO - back, tt), gs]
        mixed = acc / jnp.minimum(t_count, float(win)) - self_rows
        d = _dot(mixed.astype(BF16), pw_ref[g]) * ps_ref[:, gs]
        o_ref[:, cw_width + g * gd:cw_width + (g + 1) * gd] = d.astype(o_ref.dtype)


def odd_mix(proj, conv_w, pool_w, pool_scale, width, tt=256):
    m = proj.shape[0]
    taps = conv_w.shape[0]
    groups, gd, _ = pool_w.shape
    assert groups == len(POOL_WINDOWS) and max(POOL_WINDOWS) - 1 <= POOL_HALO and taps - 1 <= SUBLANES
    tt = min(tt, m)
    const2 = lambda i: (0, 0)
    return pl.pallas_call(
        functools.partial(_odd_mix_kernel, taps=taps),
        grid=(m // tt,),
        in_specs=[pl.BlockSpec((tt, width), lambda i: (i, 0)),
                  pl.BlockSpec((tt, width), lambda i: (i, 1)),
                  pl.BlockSpec((tt, width), lambda i: (i, 2)),
                  pl.BlockSpec((tt, width), lambda i: (i, 3)),
                  pl.BlockSpec((taps, width), const2),
                  pl.BlockSpec((groups, gd, gd), lambda i: (0, 0, 0)),
                  pl.BlockSpec((1, width), const2)],
        out_specs=pl.BlockSpec((tt, 2 * width), lambda i: (i, 0)),
        out_shape=jax.ShapeDtypeStruct((m, 2 * width), BF16),
        scratch_shapes=[pltpu.VMEM((tt + SUBLANES, width), F32),
                        pltpu.VMEM((tt + POOL_HALO, width), F32)],
        compiler_params=_params(1),
        name="odd_mix",
    )(proj, proj, proj, proj, conv_w.astype(F32), pool_w.astype(BF16), pool_scale.reshape(1, width).astype(F32))


def kernel(x, mix_norm_g, ffn_norm_g, final_norm_g, ev_w_in, ev_w_out, a_conv_w, a_conv_b, a_ln_g, a_ln_b,
           dn_conv_w, dn_a_log, dn_dt_bias, dn_norm_g, od_w_in, od_w_out, sc_conv_w, pool_w, pool_scale,
           ffn_w_gate, ffn_conv_w, ffn_w_up, ffn_w_down):
    bsz, seq, d_model = x.shape
    assert bsz == 1, "sequence mixers carry state along the row axis; one sequence per call"
    depth = mix_norm_g.shape[0]
    half = d_model // 2
    heads = dn_a_log.shape[1]
    d_ff = ffn_w_gate.shape[2]
    ff_padded = d_ff + (-d_ff) % FFN_TILE
    xs = x.reshape(seq, d_model).astype(F32)

    conv_w = jnp.pad(ffn_conv_w.astype(F32), ((0, 0), (0, 0), (0, ff_padded - d_ff)))
    ev_in = ev_w_in.astype(BF16)
    mix_w = (None, cast_weight(ev_w_out, 0))
    ffn_w = (cast_weight(ffn_w_gate, 0, out_cols=ff_padded), cast_weight(ffn_w_up, 0, out_cols=ff_padded),
             cast_weight(ffn_w_down, 0, out_rows=ff_padded))

    xg, ssq = norm_inputs(xs, mix_norm_g[0])
    for layer in range(depth):
        j = layer // 2
        w_in, w_out = mix_w
        if layer % 2 == 0:
            w_log = ev_in[j, :, 6 * half:]
            lane_pad = ((0, 0), (0, LANES - heads))
            w_log = jnp.concatenate([jnp.pad(w_log[:, :heads], lane_pad), jnp.pad(w_log[:, heads:], lane_pad)], axis=1)
            proj = proj_matmul(xg, ssq, ev_in, 6 * half, tn=1024, layer=j)
            logits = proj_matmul(xg, ssq, w_log, 2 * LANES, tn=2 * LANES)
            mixed = mixer_a(proj, a_conv_w[j], a_conv_b[j], a_ln_g[j], a_ln_b[j], half)
            mixed = deltanet(proj, logits, dn_conv_w[j], dn_a_log[j], dn_dt_bias[j], dn_norm_g[j], mixed, half, col0=2)
        else:
            proj = proj_matmul(xg, ssq, w_in, 4 * half, tn=1024)
            mixed = odd_mix(proj, sc_conv_w[j], pool_w[j], pool_scale[j], half)
        xs, xg, ssq = out_proj(mixed, w_out, xs, ffn_norm_g[layer])
        if layer + 1 == depth:
            act, _ = ffn_act(xg, ssq, ffn_w[0], ffn_w[1], conv_w[layer])
            xs, _, _ = ffn_down(act, ffn_w[2], xs, None)
        else:
            nxt = layer + 1
            casts = [(ffn_w_gate, nxt, d_model, ff_padded), (ffn_w_up, nxt, d_model, ff_padded),
                     (ffn_w_down, nxt, ff_padded, d_model)]
            if nxt % 2 == 0:
                casts += [(ev_w_out, nxt // 2) + ev_w_out.shape[1:]]
            else:
                casts += [(od_w_out, nxt // 2) + od_w_out.shape[1:], (od_w_in, nxt // 2) + od_w_in.shape[1:]]
            act, cast = ffn_act(xg, ssq, ffn_w[0], ffn_w[1], conv_w[layer], casts)
            xs, xg, ssq = ffn_down(act, ffn_w[2], xs, mix_norm_g[nxt])
            ffn_w = cast[:3]
            mix_w = (cast[4] if nxt % 2 else None, cast[3])
    return rmsnorm(xs, final_norm_g, x.dtype).reshape(bsz, seq, d_model)
```
